```python
import jax, jax.numpy as jnp
from jax import lax
import numpy as np

D_MODEL = 1024
BATCH = 4
SEQ = 4096
DEPTH = 2

RNN_WIDTH = 512
RNN_BLOCKS = 8
RNN_BLOCK = RNN_WIDTH // RNN_BLOCKS
RNN_CONV = 4
RG_C = 8.0
NSA_HEADS = 8
NSA_KV_HEADS = 2
HEAD_DIM = 64
NSA_GROUP = NSA_HEADS // NSA_KV_HEADS
NSA_WIDTH = NSA_HEADS * HEAD_DIM
KV_WIDTH = NSA_KV_HEADS * HEAD_DIM
N_NSA_BRANCHES = 3
CMP_LEN = 32
CMP_STRIDE = 16
CMP_HIDDEN = 128
SEL_LEN = 64
SEL_TOPK = 16
WINDOW = 512
Q_BLOCK = 128
SC_WIDTH = 512
SC_CONV = 3
N_MIXERS = 3
N_GROUPS = 4
EXPERTS_PER_GROUP = 8
N_EXPERTS = N_GROUPS * EXPERTS_PER_GROUP
EXPERT_HIDDEN = 256
TOPK_IN_GROUP = 2

IN_SPLITS = (RNN_WIDTH, RNN_WIDTH, NSA_WIDTH) + (KV_WIDTH,) * 6 + (NSA_HEADS * N_NSA_BRANCHES,) + (SC_WIDTH,) * 3 + (N_MIXERS * D_MODEL,)
IN_WIDTH = sum(IN_SPLITS)

RMS_EPS = 1e-6
NEG_INF = -1e30
FORCE_SCORE = 1e6

kernel_name = 'hybrid_rglru_nsa_shortconv_hmoe'


def _split_points():
    return [int(v) for v in np.cumsum(IN_SPLITS)[:-1]]


def rms_norm(x, gain):
    xf = x.astype(jnp.float32)
    y = xf * lax.rsqrt(jnp.mean(xf * xf, axis=-1, keepdims=True) + RMS_EPS)
    return (y * gain.astype(jnp.float32)).astype(x.dtype)


def causal_depthwise_conv(x, w):
    k, c = w.shape
    return lax.conv_general_dilated(
        x, w[:, None, :].astype(x.dtype), window_strides=(1,), padding=[(k - 1, 0)],
        dimension_numbers=('NWC', 'WIO', 'NWC'), feature_group_count=c)


def rg_lru(x, wa, ba, wx, bx, lam):
    b, s, _ = x.shape
    xb = x.reshape(b, s, RNN_BLOCKS, RNN_BLOCK)
    r = jax.nn.sigmoid(jnp.einsum('bsni,nij->bsnj', xb, wa) + ba).reshape(b, s, RNN_WIDTH)
    i = jax.nn.sigmoid(jnp.einsum('bsni,nij->bsnj', xb, wx) + bx).reshape(b, s, RNN_WIDTH)
    log_a = -RG_C * r.astype(jnp.float32) * jax.nn.softplus(-lam.astype(jnp.float32))
    a = jnp.exp(log_a)
    mult = jnp.sqrt(-jnp.expm1(2.0 * log_a))
    u = mult * (i * x).astype(jnp.float32)

    def combine(left, right):
        a1, b1 = left
        a2, b2 = right
        return a1 * a2, a2 * b1 + b2

    _, h = lax.associative_scan(combine, (a, u), axis=1)
    return h.astype(x.dtype)


def compress_blocks(k, pe, w1, b1, w2, b2):
    b, s = k.shape[:2]
    n_cmp = (s - CMP_LEN) // CMP_STRIDE + 1
    idx = jnp.arange(n_cmp)[:, None] * CMP_STRIDE + jnp.arange(CMP_LEN)[None, :]
    blocks = k[:, idx] + pe[None, None, :, None, :]
    blocks = blocks.transpose(0, 1, 3, 2, 4).reshape(b, n_cmp, NSA_KV_HEADS, CMP_LEN * HEAD_DIM)
    hid = jax.nn.gelu(blocks @ w1 + b1)
    return hid @ w2 + b2


def native_sparse_attention(q, k_cmp, v_cmp, k_sel, v_sel, k_win, v_win, gates):
    f32 = jnp.float32
    b, s = q.shape[:2]
    n_chunks = s // Q_BLOCK
    n_cmp = k_cmp.shape[1]
    n_sel = s // SEL_LEN
    top_k = min(SEL_TOPK, n_sel)
    scale = HEAD_DIM ** -0.5
    k_cmp = k_cmp.astype(f32)
    v_cmp = v_cmp.astype(f32)
    cmp_start = jnp.arange(n_cmp) * CMP_STRIDE
    cmp_end = cmp_start + CMP_LEN
    sel_start = jnp.arange(n_sel) * SEL_LEN
    overlap = jnp.clip(jnp.minimum(cmp_end[:, None], sel_start[None, :] + SEL_LEN)
                       - jnp.maximum(cmp_start[:, None], sel_start[None, :]), 0).astype(f32) / CMP_LEN
    ks_blk = k_sel.astype(f32).reshape(b, n_sel, SEL_LEN, NSA_KV_HEADS, HEAD_DIM).transpose(0, 3, 1, 2, 4)
    vs_blk = v_sel.astype(f32).reshape(b, n_sel, SEL_LEN, NSA_KV_HEADS, HEAD_DIM).transpose(0, 3, 1, 2, 4)
    pad = jnp.zeros((b, WINDOW, NSA_KV_HEADS, HEAD_DIM), f32)
    kw_pad = jnp.concatenate([pad, k_win.astype(f32)], axis=1)
    vw_pad = jnp.concatenate([pad, v_win.astype(f32)], axis=1)
    bi = jnp.arange(b)[:, None, None, None]
    gi = jnp.arange(NSA_KV_HEADS)[None, :, None, None]
    j = jnp.arange(n_sel)

    def query_block(c):
        t0 = c * Q_BLOCK
        t = t0 + jnp.arange(Q_BLOCK)
        qc = lax.dynamic_slice_in_dim(q, t0, Q_BLOCK, axis=1).astype(f32)
        s_c = jnp.einsum('btghd,bngd->bghtn', qc, k_cmp) * scale
        m_c = cmp_end[None, :] - 1 <= t[:, None]
        p_c = jax.nn.softmax(jnp.where(m_c, s_c, NEG_INF), axis=-1)
        p_c = p_c * jnp.any(m_c, axis=-1)[:, None].astype(f32)
        o_c = jnp.einsum('bghtn,bngd->btghd', p_c, v_cmp)
        imp = jnp.einsum('bghtn,nj->bgtj', p_c, overlap)
        cur = t // SEL_LEN
        causal_j = j[None, :] <= cur[:, None]
        forced = (j[None, :] == 0) | (j[None, :] == cur[:, None]) | (j[None, :] == cur[:, None] - 1)
        imp = jnp.where(forced, FORCE_SCORE, imp)
        imp = jnp.where(causal_j, imp, -1.0)
        top_val, top_idx = lax.top_k(imp, top_k)
        blk_ok = top_val >= 0.0
        kg = ks_blk[bi, gi, top_idx]
        vg = vs_blk[bi, gi, top_idx]
        pos = top_idx[..., None] * SEL_LEN + jnp.arange(SEL_LEN)
        m_s = blk_ok[..., None] & (pos <= t[None, None, :, None, None])
        s_s = jnp.einsum('btghd,bgtkld->bghtkl', qc, kg) * scale
        s_s = jnp.where(m_s[:, :, None], s_s, NEG_INF).reshape(b, NSA_KV_HEADS, NSA_GROUP, Q_BLOCK, top_k * SEL_LEN)
        p_s = jax.nn.softmax(s_s, axis=-1).reshape(b, NSA_KV_HEADS, NSA_GROUP, Q_BLOCK, top_k, SEL_LEN)
        o_s = jnp.einsum('bghtkl,bgtkld->btghd', p_s, vg)
        kwc = lax.dynamic_slice_in_dim(kw_pad, t0, Q_BLOCK + WINDOW, axis=1)
        vwc = lax.dynamic_slice_in_dim(vw_pad, t0, Q_BLOCK + WINDOW, axis=1)
        spos = t0 - WINDOW + jnp.arange(Q_BLOCK + WINDOW)
        diff = t[:, None] - spos[None, :]
        m_w = (diff >= 0) & (diff < WINDOW) & (spos[None, :] >= 0)
        s_w = jnp.einsum('btghd,bsgd->bghts', qc, kwc) * scale
        p_w = jax.nn.softmax(jnp.where(m_w, s_w, NEG_INF), axis=-1)
        o_w = jnp.einsum('bghts,bsgd->btghd', p_w, vwc)
        gc = lax.dynamic_slice_in_dim(gates, t0, Q_BLOCK, axis=1).astype(f32)
        o = gc[..., 0:1] * o_c + gc[..., 1:2] * o_s + gc[..., 2:3] * o_w
        return o.reshape(b, Q_BLOCK, NSA_WIDTH)

    out = lax.map(query_block, jnp.arange(n_chunks))
    return out.transpose(1, 0, 2, 3).reshape(b, s, NSA_WIDTH).astype(q.dtype)


def hierarchical_moe(x, wg, bg, we, be, w_gate, w_up, w_down):
    f32 = jnp.float32
    b, s, d = x.shape
    n = b * s
    xt = x.reshape(n, d)
    g_logits = (xt @ wg).astype(f32) + bg.astype(f32)
    g_prob = jax.nn.softmax(g_logits, axis=-1)
    grp = jnp.argmax(g_logits, axis=-1)
    tok = jnp.arange(n)
    p_grp = g_prob[tok, grp][:, None]
    e_logits = ((xt @ we).astype(f32) + be.astype(f32)).reshape(n, N_GROUPS, EXPERTS_PER_GROUP)
    e_prob = jax.nn.softmax(e_logits[tok, grp], axis=-1)
    top_p, top_i = lax.top_k(e_prob, TOPK_IN_GROUP)
    weights = p_grp * top_p / jnp.sum(top_p, axis=-1, keepdims=True)
    expert_id = grp[:, None] * EXPERTS_PER_GROUP + top_i
    combine = jnp.einsum('nke,nk->ne', jax.nn.one_hot(expert_id, N_EXPERTS, dtype=f32), weights)
    y = jnp.zeros((n, d), f32)
    for g in range(N_GROUPS):
        e0 = g * EXPERTS_PER_GROUP
        e1 = e0 + EXPERTS_PER_GROUP
        h = jax.nn.silu(jnp.einsum('nd,edf->nef', xt, w_gate[e0:e1])) * jnp.einsum('nd,edf->nef', xt, w_up[e0:e1])
        h = h * combine[:, e0:e1, None].astype(h.dtype)
        y = y + jnp.einsum('nef,efd->nd', h, w_down[e0:e1])
    return y.reshape(b, s, d).astype(x.dtype)


def setup_inputs(seed: int = 0) -> dict:
    key = jax.random.key(seed)
    ks = jax.random.split(key, 30)
    L = DEPTH

    def nrm(k, shape, scale):
        return jax.random.normal(k, shape, jnp.float32) * scale

    u = jax.random.uniform(ks[9], (L, RNN_WIDTH), jnp.float32, 0.9, 0.999)
    p = u ** (1.0 / RG_C)
    return {
        'x': nrm(ks[0], (BATCH, SEQ, D_MODEL), 1.0),
        'mix_norm': 1.0 + nrm(ks[1], (L, D_MODEL), 0.05),
        'w_in': nrm(ks[2], (L, D_MODEL, IN_WIDTH), D_MODEL ** -0.5),
        'rnn_conv_w': nrm(ks[3], (L, RNN_CONV, RNN_WIDTH), RNN_CONV ** -0.5),
        'rnn_conv_b': nrm(ks[4], (L, RNN_WIDTH), 0.01),
        'rg_wa': nrm(ks[5], (L, RNN_BLOCKS, RNN_BLOCK, RNN_BLOCK), RNN_BLOCK ** -0.5),
        'rg_ba': nrm(ks[6], (L, RNN_BLOCKS, RNN_BLOCK), 0.01),
        'rg_wx': nrm(ks[7], (L, RNN_BLOCKS, RNN_BLOCK, RNN_BLOCK), RNN_BLOCK ** -0.5),
        'rg_bx': nrm(ks[8], (L, RNN_BLOCKS, RNN_BLOCK), 0.01),
        'rg_lambda': jnp.log(p) - jnp.log1p(-p),
        'cmp_pe': nrm(ks[10], (L, 2, CMP_LEN, HEAD_DIM), 0.1),
        'cmp_w1': nrm(ks[11], (L, 2, CMP_LEN * HEAD_DIM, CMP_HIDDEN), (CMP_LEN * HEAD_DIM) ** -0.5),
        'cmp_b1': nrm(ks[12], (L, 2, CMP_HIDDEN), 0.01),
        'cmp_w2': nrm(ks[13], (L, 2, CMP_HIDDEN, HEAD_DIM), CMP_HIDDEN ** -0.5),
        'cmp_b2': nrm(ks[14], (L, 2, HEAD_DIM), 0.01),
        'q_norm': 1.0 + nrm(ks[15], (L, HEAD_DIM), 0.05),
        'k_norm': 1.0 + nrm(ks[16], (L, N_NSA_BRANCHES, HEAD_DIM), 0.05),
        'sc_conv_w': nrm(ks[17], (L, SC_CONV, SC_WIDTH), SC_CONV ** -0.5),
        'w_rnn_out': nrm(ks[18], (L, RNN_WIDTH, D_MODEL), RNN_WIDTH ** -0.5),
        'w_nsa_out': nrm(ks[19], (L, NSA_WIDTH, D_MODEL), NSA_WIDTH ** -0.5),
        'w_sc_out': nrm(ks[20], (L, SC_WIDTH, D_MODEL), SC_WIDTH ** -0.5),
        'w_out': nrm(ks[21], (L, D_MODEL, D_MODEL), D_MODEL ** -0.5),
        'ffn_norm': 1.0 + nrm(ks[22], (L, D_MODEL), 0.05),
        'router_group_w': nrm(ks[23], (L, D_MODEL, N_GROUPS), D_MODEL ** -0.5),
        'router_group_b': nrm(ks[24], (L, N_GROUPS), 0.01),
        'router_expert_w': nrm(ks[25], (L, D_MODEL, N_EXPERTS), D_MODEL ** -0.5),
        'router_expert_b': nrm(ks[26], (L, N_EXPERTS), 0.01),
        'exp_w_gate': nrm(ks[27], (L, N_EXPERTS, D_MODEL, EXPERT_HIDDEN), D_MODEL ** -0.5),
        'exp_w_up': nrm(ks[28], (L, N_EXPERTS, D_MODEL, EXPERT_HIDDEN), D_MODEL ** -0.5),
        'exp_w_down': nrm(ks[29], (L, N_EXPERTS, EXPERT_HIDDEN, D_MODEL), EXPERT_HIDDEN ** -0.5),
    }


def reference(x, mix_norm, w_in, rnn_conv_w, rnn_conv_b, rg_wa, rg_ba, rg_wx, rg_bx, rg_lambda,
              cmp_pe, cmp_w1, cmp_b1, cmp_w2, cmp_b2, q_norm, k_norm, sc_conv_w,
              w_rnn_out, w_nsa_out, w_sc_out, w_out, ffn_norm,
              router_group_w, router_group_b, router_expert_w, router_expert_b,
              exp_w_gate, exp_w_up, exp_w_down):
    b, s, _ = x.shape
    split_points = _split_points()
    for l in range(DEPTH):
        xn = rms_norm(x, mix_norm[l])
        (rnn_x, rnn_y, q, kc, vc, ksl, vsl, kwn, vwn, nsa_g,
         sc_b, sc_c, sc_h, merge_g) = jnp.split(xn @ w_in[l], split_points, axis=-1)

        u = causal_depthwise_conv(rnn_x, rnn_conv_w[l]) + rnn_conv_b[l]
        y_rnn = jax.nn.gelu(rnn_y) * rg_lru(u, rg_wa[l], rg_ba[l], rg_wx[l], rg_bx[l], rg_lambda[l])

        qh = rms_norm(q.reshape(b, s, NSA_HEADS, HEAD_DIM), q_norm[l])
        qh = qh.reshape(b, s, NSA_KV_HEADS, NSA_GROUP, HEAD_DIM)
        kv_shape = (b, s, NSA_KV_HEADS, HEAD_DIM)
        k_cmp = rms_norm(compress_blocks(kc.reshape(kv_shape), cmp_pe[l, 0], cmp_w1[l, 0], cmp_b1[l, 0],
                                         cmp_w2[l, 0], cmp_b2[l, 0]), k_norm[l, 0])
        v_cmp = compress_blocks(vc.reshape(kv_shape), cmp_pe[l, 1], cmp_w1[l, 1], cmp_b1[l, 1],
                                cmp_w2[l, 1], cmp_b2[l, 1])
        k_sel = rms_norm(ksl.reshape(kv_shape), k_norm[l, 1])
        k_win = rms_norm(kwn.reshape(kv_shape), k_norm[l, 2])
        nsa_gates = jax.nn.sigmoid(nsa_g).reshape(b, s, NSA_KV_HEADS, NSA_GROUP, N_NSA_BRANCHES)
        y_nsa = native_sparse_attention(qh, k_cmp, v_cmp, k_sel, vsl.reshape(kv_shape),
                                        k_win, vwn.reshape(kv_shape), nsa_gates)

        y_sc = sc_b * causal_depthwise_conv(sc_c * sc_h, sc_conv_w[l])

        gates = jax.nn.sigmoid(merge_g).reshape(b, s, N_MIXERS, D_MODEL)
        merged = (gates[:, :, 0] * (y_rnn @ w_rnn_out[l])
                  + gates[:, :, 1] * (y_nsa @ w_nsa_out[l])
                  + gates[:, :, 2] * (y_sc @ w_sc_out[l]))
        x = x + merged @ w_out[l]

        x = x + hierarchical_moe(rms_norm(x, ffn_norm[l]), router_group_w[l], router_group_b[l],
                                 router_expert_w[l], router_expert_b[l],
                                 exp_w_gate[l], exp_w_up[l], exp_w_down[l])
    return x
```

```python
import functools

import jax
import jax.numpy as jnp
from jax import lax
from jax.experimental import pallas as pl
from jax.experimental.pallas import tpu as pltpu

F32 = jnp.float32
BF16 = jnp.bfloat16

RMS_EPS = 1e-6
NEG_INF = -1e30
FORCE_SCORE = 1e6
RG_C = 8.0
HEAD_DIM = 64
NSA_GROUP = 4
N_NSA_BRANCHES = 3
CMP_LEN = 32
CMP_STRIDE = 16
SEL_LEN = 64
SEL_TOPK = 16
WINDOW = 512
Q_BLOCK = 128
EXPERTS_PER_GROUP = 8
TOPK_IN_GROUP = 2

LANES = 128
SUBLANES = 8
COL_BLOCK = 512
VMEM_LIMIT = 56 * 1024 * 1024


def _cparams(*sem):
    return pltpu.CompilerParams(dimension_semantics=sem, vmem_limit_bytes=VMEM_LIMIT)


def _rms(x, gain):
    return x * lax.rsqrt(jnp.mean(x * x, axis=-1, keepdims=True) + RMS_EPS) * gain


def _gelu_tanh(x):
    return 0.5 * x * (1.0 + jnp.tanh(0.7978845608028654 * (x + 0.044715 * x * x * x)))


def _dot(a, b):
    return jnp.dot(a, b, preferred_element_type=F32)


def _dot_nt(a, b):
    return lax.dot_general(a, b, (((1,), (1,)), ((), ())), preferred_element_type=F32)


def _inproj_kernel(x_ref, g_ref, w_ref, o_ref):
    xn = _rms(x_ref[...], g_ref[...]).astype(BF16)
    for c in range(w_ref.shape[1] // COL_BLOCK):
        sl = slice(c * COL_BLOCK, (c + 1) * COL_BLOCK)
        o_ref[:, sl] = _dot(xn, w_ref[:, sl]).astype(BF16)


def _inproj(x2, gain, w_packed, tm=512):
    n, d = x2.shape
    wp = w_packed.shape[1]
    return pl.pallas_call(
        _inproj_kernel,
        grid=(n // tm,),
        in_specs=[
            pl.BlockSpec((tm, d), lambda i: (i, 0)),
            pl.BlockSpec((1, d), lambda i: (0, 0)),
            pl.BlockSpec((d, wp), lambda i: (0, 0), pipeline_mode=pl.Buffered(1)),
        ],
        out_specs=pl.BlockSpec((tm, wp), lambda i: (i, 0)),
        out_shape=jax.ShapeDtypeStruct((n, wp), BF16),
        compiler_params=_cparams("parallel"),
        name="inproj",
    )(x2, gain, w_packed)


def _rnn_kernel(x_ref, xp_ref, y_ref, cw_ref, cb_ref, wa_ref, ba_ref, wx_ref, bx_ref, lam_ref,
                o_ref, h_scr):
    ti = pl.program_id(1)
    ts = x_ref.shape[0]
    kw = cw_ref.shape[0]

    @pl.when(ti == 0)
    def _():
        h_scr[...] = jnp.zeros_like(h_scr)

    x = x_ref[...].astype(F32)
    halo = jnp.where(ti == 0, 0.0, xp_ref[...].astype(F32))
    xe = jnp.concatenate([halo, x], axis=0)
    u = cb_ref[...]
    for k in range(kw):
        off = SUBLANES - (kw - 1) + k
        u = u + cw_ref[k:k + 1, :] * xe[off:off + ts, :]
    ub = u.astype(BF16)
    r = jax.nn.sigmoid(_dot(ub, wa_ref[...]) + ba_ref[...])
    gi = jax.nn.sigmoid(_dot(ub, wx_ref[...]) + bx_ref[...])
    lam = lam_ref[...]
    softplus_neg = jnp.maximum(-lam, 0.0) + jnp.log(1.0 + jnp.exp(-jnp.abs(lam)))
    log_a = -RG_C * r * softplus_neg
    a = jnp.exp(log_a)
    b = jnp.sqrt(1.0 - jnp.exp(2.0 * log_a)) * (gi * u)

    row = lax.broadcasted_iota(jnp.int32, a.shape, 0)
    d = 1
    while d < ts:
        keep = row >= d
        a_sh = pltpu.roll(a, d, axis=0)
        b_sh = pltpu.roll(b, d, axis=0)
        b = jnp.where(keep, a * b_sh + b, b)
        a = jnp.where(keep, a * a_sh, a)
        d *= 2
    h = a * h_scr[...] + b
    h_scr[...] = h[ts - 1:ts, :]
    o_ref[...] = (_gelu_tanh(y_ref[...].astype(F32)) * h).astype(BF16)


def _rnn_branch(proj3, conv_w, conv_b, wa_bd, ba, wx_bd, bx, lam, ts=512):
    b, s, _ = proj3.shape
    w = conv_w.shape[1]
    nb = w // COL_BLOCK
    rows_per = ts // SUBLANES
    const = lambda shape: pl.BlockSpec(shape, lambda bi, ti: (0, 0))
    return pl.pallas_call(
        _rnn_kernel,
        grid=(b, s // ts),
        in_specs=[
            pl.BlockSpec((None, ts, w), lambda bi, ti: (bi, ti, 0)),
            pl.BlockSpec((None, SUBLANES, w), lambda bi, ti: (bi, jnp.maximum(ti * rows_per - 1, 0), 0)),
            pl.BlockSpec((None, ts, w), lambda bi, ti: (bi, ti, nb)),
            const(conv_w.shape), const((1, w)), const((w, w)), const((1, w)), const((w, w)), const((1, w)),
            const((1, w)),
        ],
        out_specs=pl.BlockSpec((None, ts, w), lambda bi, ti: (bi, ti, 0)),
        out_shape=jax.ShapeDtypeStruct((b, s, w), BF16),
        scratch_shapes=[pltpu.VMEM((1, w), F32)],
        compiler_params=_cparams("parallel", "arbitrary"),
        name="rnn_branch",
    )(proj3, proj3, proj3, conv_w, conv_b, wa_bd, ba, wx_bd, bx, lam)


def _kvprep_kernel(kcf_ref, vcf_ref, ks_ref, kw_ref, pe_ref, w1_ref, b1_ref, w2_ref, b2_ref, kn_ref,
                   kc_o, vc_o, ks_o, kw_o):
    half = kcf_ref.shape[1]
    nrow = kcf_ref.shape[0]

    def compress(flat_ref, i):
        xf = flat_ref[...].astype(F32)
        xa = (xf + pe_ref[i, 0:1, :]).astype(BF16)
        xb = (xf + pe_ref[i, 1:2, :]).astype(BF16)
        first = _dot(xa, w1_ref[i, 0:half, :])
        second = _dot(xb, w1_ref[i, half:2 * half, :])
        hid = first + pltpu.roll(second, nrow - 1, axis=0) + b1_ref[i]
        hid = _gelu_tanh(hid).astype(BF16)
        return _dot(hid, w2_ref[i]) + b2_ref[i]

    kc_o[...] = _rms(compress(kcf_ref, 0), kn_ref[0:1, :]).astype(BF16)
    vc_o[...] = compress(vcf_ref, 1).astype(BF16)
    ks_o[...] = _rms(ks_ref[...].astype(F32), kn_ref[1:2, :]).astype(BF16)
    kw_o[...] = _rms(kw_ref[...].astype(F32), kn_ref[2:3, :]).astype(BF16)


def _kvprep(kc_flat, vc_flat, ks, kw, pe_flat, w1, b1, w2, b2, k_norm):
    b, g, nrow, half = kc_flat.shape
    s = ks.shape[2]
    dh = ks.shape[3]
    bg = lambda shape: pl.BlockSpec((None, None) + shape, lambda bi, gi: (bi, gi, 0, 0))
    full = lambda a: pl.BlockSpec(a.shape, lambda bi, gi: (0,) * a.ndim)
    return pl.pallas_call(
        _kvprep_kernel,
        grid=(b, g),
        in_specs=[bg((nrow, half)), bg((nrow, half)), bg((s, dh)), bg((s, dh)),
                  full(pe_flat), full(w1), full(b1), full(w2), full(b2), full(k_norm)],
        out_specs=[bg((nrow, dh)), bg((nrow, dh)), bg((s, dh)), bg((s, dh))],
        out_shape=[jax.ShapeDtypeStruct((b, g, nrow, dh), BF16), jax.ShapeDtypeStruct((b, g, nrow, dh), BF16),
                   jax.ShapeDtypeStruct((b, g, s, dh), BF16), jax.ShapeDtypeStruct((b, g, s, dh), BF16)],
        compiler_params=_cparams("parallel", "parallel"),
        name="kvprep",
    )(kc_flat, vc_flat, ks, kw, pe_flat, w1, b1, w2, b2, k_norm)


def _softmax_rows(s):
    m = jnp.max(s, axis=-1, keepdims=True)
    e = jnp.exp(s - m)
    return e / jnp.sum(e, axis=-1, keepdims=True)


def _nsa_kernel(q_ref, kc_ref, vc_ref, ks_ref, vs_ref, kw_ref, vw_ref, g_ref, qn_ref, ov_ref, o_ref,
                *, n_cmp, key_tile):
    c = pl.program_id(2)
    tq = Q_BLOCK
    t0 = c * tq
    n_sel = ov_ref.shape[0]
    ncp = kc_ref.shape[0]
    top_k = min(SEL_TOPK, n_sel)
    blocks_per_tile = key_tile // SEL_LEN

    q = _rms(q_ref[...].astype(F32), qn_ref[...]) * (HEAD_DIM ** -0.5)
    qs = q.astype(BF16)

    s_c = _dot_nt(qs, kc_ref[...])
    row = lax.broadcasted_iota(jnp.int32, s_c.shape, 0)
    t_c = t0 + (row & (tq - 1))
    n_i = lax.broadcasted_iota(jnp.int32, s_c.shape, 1)
    m_c = (n_i * CMP_STRIDE + (CMP_LEN - 1) <= t_c) & (n_i < n_cmp)
    p_c = _softmax_rows(jnp.where(m_c, s_c, NEG_INF))
    p_c = jnp.where(t_c >= CMP_LEN - 1, p_c, 0.0)
    o_c = _dot(p_c.astype(BF16), vc_ref[...])

    ps = p_c[0:tq]
    for h in range(1, NSA_GROUP):
        ps = ps + p_c[h * tq:(h + 1) * tq]
    ps_hi = ps.astype(BF16)
    ps_lo = (ps - ps_hi.astype(F32)).astype(BF16)
    imp = _dot_nt(ov_ref[...], ps_hi) + _dot_nt(ov_ref[...], ps_lo)
    j_i = lax.broadcasted_iota(jnp.int32, imp.shape, 0)
    t_i = t0 + lax.broadcasted_iota(jnp.int32, imp.shape, 1)
    cur = t_i // SEL_LEN
    forced = (j_i == 0) | (j_i == cur) | (j_i == cur - 1)
    imp = jnp.where(forced, FORCE_SCORE, imp)
    imp = jnp.where(j_i <= cur, imp, -1.0)
    cnt = jnp.zeros(imp.shape, jnp.int32)
    for jp in range(n_sel):
        r = imp[jp:jp + 1, :]
        before = (r > imp) | ((r == imp) & (j_i > jp))
        cnt = cnt + before.astype(jnp.int32)
    sel_t = ((cnt < top_k) & (imp >= 0.0)).astype(F32)
    sel = sel_t.T.astype(BF16)

    n_tiles = (t0 + tq + key_tile - 1) // key_tile
    tq_i = t0 + lax.broadcasted_iota(jnp.int32, (tq, key_tile), 0)
    kcol = lax.broadcasted_iota(jnp.int32, (tq, key_tile), 1)
    e_row = lax.broadcasted_iota(jnp.int32, (n_sel, key_tile), 0)
    e_col = lax.broadcasted_iota(jnp.int32, (n_sel, key_tile), 1) // SEL_LEN

    def sel_body(kt, carry):
        k0 = pl.multiple_of(kt * key_tile, key_tile)
        k_t = ks_ref[pl.ds(k0, key_tile), :]
        v_t = vs_ref[pl.ds(k0, key_tile), :]
        expand = (e_row == kt * blocks_per_tile + e_col).astype(BF16)
        picked = _dot(sel, expand)
        mask = (picked > 0.5) & (k0 + kcol <= tq_i)
        out = []
        for h in range(NSA_GROUP):
            m_o, l_o, a_o = carry[h]
            s = jnp.where(mask, _dot_nt(qs[h * tq:(h + 1) * tq], k_t), NEG_INF)
            m_n = jnp.maximum(m_o, jnp.max(s, axis=-1, keepdims=True))
            alpha = jnp.exp(m_o - m_n)
            p = jnp.exp(s - m_n)
            l_n = alpha * l_o + jnp.sum(p, axis=-1, keepdims=True)
            a_n = alpha * a_o + _dot(p.astype(BF16), v_t)
            out.append((m_n, l_n, a_n))
        return tuple(out)

    init = tuple((jnp.full((tq, 1), NEG_INF, F32), jnp.zeros((tq, 1), F32), jnp.zeros((tq, HEAD_DIM), F32))
                 for _ in range(NSA_GROUP))
    fin = lax.fori_loop(0, n_tiles, sel_body, init)

    w0 = pl.multiple_of(jnp.maximum(t0 - WINDOW, 0), tq)
    k_w = kw_ref[pl.ds(w0, WINDOW + tq), :]
    v_w = vw_ref[pl.ds(w0, WINDOW + tq), :]
    dist = (t0 + lax.broadcasted_iota(jnp.int32, (tq, WINDOW + tq), 0)) - (
        w0 + lax.broadcasted_iota(jnp.int32, (tq, WINDOW + tq), 1))
    m_w = (dist >= 0) & (dist < WINDOW)

    gates = jax.nn.sigmoid(g_ref[...].astype(F32))
    for h in range(NSA_GROUP):
        hs = slice(h * tq, (h + 1) * tq)
        _, l_s, a_s = fin[h]
        o_s = a_s / l_s
        p_w = _softmax_rows(jnp.where(m_w, _dot_nt(qs[hs], k_w), NEG_INF))
        o_w = _dot(p_w.astype(BF16), v_w)
        gb = h * N_NSA_BRANCHES
        o = gates[:, gb:gb + 1] * o_c[hs] + gates[:, gb + 1:gb + 2] * o_s + gates[:, gb + 2:gb + 3] * o_w
        o_ref[:, h * HEAD_DIM:(h + 1) * HEAD_DIM] = o.astype(BF16)


def _nsa(q5, k_cmp, v_cmp, k_sel, v_sel, k_win, v_win, gates4, q_norm, overlap_t, n_cmp, key_tile=512):
    b, g, nchunk, rows, dh = q5.shape
    s = k_sel.shape[2]
    ncp = k_cmp.shape[2]
    ng = gates4.shape[3]
    bg = lambda shape: pl.BlockSpec((None, None) + shape, lambda bi, gi, ci: (bi, gi, 0, 0))
    return pl.pallas_call(
        functools.partial(_nsa_kernel, n_cmp=n_cmp, key_tile=key_tile),
        grid=(b, g, nchunk),
        in_specs=[
            pl.BlockSpec((None, None, None, rows, dh), lambda bi, gi, ci: (bi, gi, ci, 0, 0)),
            bg((ncp, dh)), bg((ncp, dh)), bg((s, dh)), bg((s, dh)), bg((s, dh)), bg((s, dh)),
            pl.BlockSpec((None, None, Q_BLOCK, ng), lambda bi, gi, ci: (bi, gi, ci, 0)),
            pl.BlockSpec(q_norm.shape, lambda bi, gi, ci: (0, 0)),
            pl.BlockSpec(overlap_t.shape, lambda bi, gi, ci: (0, 0)),
        ],
        out_specs=pl.BlockSpec((None, Q_BLOCK, NSA_GROUP * dh), lambda bi, gi, ci: (bi, ci, gi)),
        out_shape=jax.ShapeDtypeStruct((b, s, g * NSA_GROUP * dh), BF16),
        compiler_params=_cparams("parallel", "parallel", "arbitrary"),
        name="nsa",
    )(q5, k_cmp, v_cmp, k_sel, v_sel, k_win, v_win, gates4, q_norm, overlap_t)


def _merge_kernel(x_ref, yr_ref, yn_ref, sb_ref, sc_ref, sh_ref, scp_ref, shp_ref, g0_ref, g1_ref, g2_ref,
                  cw_ref, wr_ref, wn_ref, ws_ref, wo_ref, o_ref):
    ti = pl.program_id(1)
    tm = x_ref.shape[0]
    kw = cw_ref.shape[0]
    prod = sc_ref[...].astype(F32) * sh_ref[...].astype(F32)
    halo = jnp.where(ti == 0, 0.0, scp_ref[...].astype(F32) * shp_ref[...].astype(F32))
    pe = jnp.concatenate([halo, prod], axis=0)
    conv = jnp.zeros_like(prod)
    for k in range(kw):
        off = SUBLANES - (kw - 1) + k
        conv = conv + cw_ref[k:k + 1, :] * pe[off:off + tm, :]
    y_sc = (sb_ref[...].astype(F32) * conv).astype(BF16)
    merged = (jax.nn.sigmoid(g0_ref[...].astype(F32)) * _dot(yr_ref[...], wr_ref[...])
              + jax.nn.sigmoid(g1_ref[...].astype(F32)) * _dot(yn_ref[...], wn_ref[...])
              + jax.nn.sigmoid(g2_ref[...].astype(F32)) * _dot(y_sc, ws_ref[...]))
    o_ref[...] = x_ref[...] + _dot(merged.astype(BF16), wo_ref[...])


def _merge(x3, proj3, y_rnn, y_nsa, sc_conv_w, w_rnn_out, w_nsa_out, w_sc_out, w_out, sc_block, gate_block, tm=256):
    b, s, d = x3.shape
    w = y_rnn.shape[2]
    rows_per = tm // SUBLANES
    tile = lambda width, col: pl.BlockSpec((None, tm, width), lambda bi, ti: (bi, ti, col))
    prev = lambda col: pl.BlockSpec((None, SUBLANES, w),
                                    lambda bi, ti: (bi, jnp.maximum(ti * rows_per - 1, 0), col))
    const = lambda a: pl.BlockSpec(a.shape, lambda bi, ti: (0, 0))
    return pl.pallas_call(
        _merge_kernel,
        grid=(b, s // tm),
        in_specs=[
            tile(d, 0), tile(w, 0), tile(w, 0),
            tile(w, sc_block), tile(w, sc_block + 1), tile(w, sc_block + 2),
            prev(sc_block + 1), prev(sc_block + 2),
            tile(d, gate_block), tile(d, gate_block + 1), tile(d, gate_block + 2),
            const(sc_conv_w), const(w_rnn_out), const(w_nsa_out), const(w_sc_out), const(w_out),
        ],
        out_specs=tile(d, 0),
        out_shape=jax.ShapeDtypeStruct((b, s, d), F32),
        compiler_params=_cparams("parallel", "parallel"),
        name="merge",
    )(x3, y_rnn, y_nsa, proj3, proj3, proj3, proj3, proj3, proj3, proj3, proj3,
      sc_conv_w, w_rnn_out, w_nsa_out, w_sc_out, w_out)


def _route(logits, n_experts, n_groups):
    lane_i = lax.broadcasted_iota(jnp.int32, logits.shape, 1)
    lane = lane_i.astype(F32)
    big = float(LANES)
    is_g = (lane_i >= n_experts) & (lane_i < n_experts + n_groups)
    gl = jnp.where(is_g, logits, NEG_INF)
    gmax = jnp.max(gl, axis=-1, keepdims=True)
    grp = jnp.min(jnp.where(is_g & (gl == gmax), lane - n_experts, big), axis=-1, keepdims=True)
    p_grp = 1.0 / jnp.sum(jnp.where(is_g, jnp.exp(gl - gmax), 0.0), axis=-1, keepdims=True)
    in_g = (lane_i < n_experts) & ((lane_i // EXPERTS_PER_GROUP).astype(F32) == grp)
    el = jnp.where(in_g, logits, NEG_INF)
    ee = jnp.where(in_g, jnp.exp(el - jnp.max(el, axis=-1, keepdims=True)), 0.0)
    ep = ee / jnp.sum(ee, axis=-1, keepdims=True)
    top1 = jnp.max(jnp.where(in_g, ep, -1.0), axis=-1, keepdims=True)
    i1 = jnp.min(jnp.where(in_g & (ep == top1), lane, big), axis=-1, keepdims=True)
    rest = jnp.where(in_g & (lane != i1), ep, -1.0)
    top2 = jnp.max(rest, axis=-1, keepdims=True)
    i2 = jnp.min(jnp.where(rest == top2, lane, big), axis=-1, keepdims=True)
    denom = top1 + top2
    return (jnp.where(lane == i1, p_grp * top1 / denom, 0.0)
            + jnp.where(lane == i2, p_grp * top2 / denom, 0.0))


def _moe_kernel(x_ref, g_ref, wr_ref, br_ref, wg_ref, wu_ref, wd_ref, o_ref, xn_scr, cw_scr, acc_scr,
                *, n_experts, n_groups):
    e = pl.program_id(1)

    @pl.when(e == 0)
    def _():
        xn = _rms(x_ref[...], g_ref[...]).astype(BF16)
        xn_scr[...] = xn
        cw_scr[...] = _route(_dot(xn, wr_ref[...]) + br_ref[...], n_experts, n_groups)
        acc_scr[...] = jnp.zeros_like(acc_scr)

    xn = xn_scr[...]
    lane = lax.broadcasted_iota(jnp.int32, cw_scr.shape, 1)
    cw = jnp.sum(jnp.where(lane == e, cw_scr[...], 0.0), axis=-1, keepdims=True)
    h = jax.nn.silu(_dot(xn, wg_ref[...])) * _dot(xn, wu_ref[...]) * cw
    acc_scr[...] += _dot(h.astype(BF16), wd_ref[...])

    @pl.when(e == n_experts - 1)
    def _():
        o_ref[...] = x_ref[...] + acc_scr[...]


def _moe(x2, gain, w_router, b_router, w_gate, w_up, w_down, n_groups, tm=512):
    n, d = x2.shape
    n_experts, _, f = w_gate.shape
    const = lambda a: pl.BlockSpec(a.shape, lambda i, e: (0, 0))
    return pl.pallas_call(
        functools.partial(_moe_kernel, n_experts=n_experts, n_groups=n_groups),
        grid=(n // tm, n_experts),
        in_specs=[
            pl.BlockSpec((tm, d), lambda i, e: (i, 0)),
            const(gain), const(w_router), const(b_router),
            pl.BlockSpec((None, d, f), lambda i, e: (e, 0, 0)),
            pl.BlockSpec((None, d, f), lambda i, e: (e, 0, 0)),
            pl.BlockSpec((None, f, d), lambda i, e: (e, 0, 0)),
        ],
        out_specs=pl.BlockSpec((tm, d), lambda i, e: (i, 0)),
        out_shape=jax.ShapeDtypeStruct((n, d), F32),
        scratch_shapes=[pltpu.VMEM((tm, d), BF16), pltpu.VMEM((tm, LANES), F32), pltpu.VMEM((tm, d), F32)],
        compiler_params=_cparams("parallel", "arbitrary"),
        name="moe",
    )(x2, gain, w_router, b_router, w_gate, w_up, w_down)


def _block_diag(w):
    nb, bs, _ = w.shape
    eye = jnp.eye(nb, dtype=w.dtype)
    return (eye[:, None, :, None] * w[:, :, None, :]).reshape(nb * bs, nb * bs)


def _overlap_t(n_sel, ncp):
    n = jnp.arange(ncp)
    j = jnp.arange(n_sel)
    c0 = n * CMP_STRIDE
    s0 = j * SEL_LEN
    ov = jnp.clip(jnp.minimum(c0[None, :] + CMP_LEN, s0[:, None] + SEL_LEN)
                  - jnp.maximum(c0[None, :], s0[:, None]), 0).astype(F32) / CMP_LEN
    return ov.astype(BF16)


def kernel(x, mix_norm, w_in, rnn_conv_w, rnn_conv_b, rg_wa, rg_ba, rg_wx, rg_bx, rg_lambda, cmp_pe, cmp_w1, cmp_b1, cmp_w2, cmp_b2, q_norm, k_norm, sc_conv_w, w_rnn_out, w_nsa_out, w_sc_out, w_out, ffn_norm, router_group_w, router_group_b, router_expert_w, router_expert_b, exp_w_gate, exp_w_up, exp_w_down):
    b, s, d = x.shape
    depth = w_in.shape[0]
    rnn_w = rnn_conv_w.shape[2]
    sc_w = sc_conv_w.shape[2]
    nsa_w = w_nsa_out.shape[1]
    dh = q_norm.shape[1]
    n_heads = nsa_w // dh
    kvh = n_heads // NSA_GROUP
    kv_w = kvh * dh
    n_groups = router_group_w.shape[2]
    n_experts = router_expert_w.shape[2]
    n = b * s
    assert rnn_w == COL_BLOCK and sc_w == COL_BLOCK and nsa_w == COL_BLOCK and dh == HEAD_DIM
    assert 4 * kv_w == COL_BLOCK and d % COL_BLOCK == 0
    assert s % COL_BLOCK == 0 and s >= WINDOW + Q_BLOCK and s % SEL_LEN == 0
    assert n_experts + n_groups <= LANES and n_experts == n_groups * EXPERTS_PER_GROUP

    n_cmp = (s - CMP_LEN) // CMP_STRIDE + 1
    nrow = s // CMP_STRIDE
    n_sel = s // SEL_LEN
    n_gate = n_heads * N_NSA_BRANCHES
    head_cols = 3 * COL_BLOCK + 6 * kv_w + n_gate
    pad_cols = 5 * COL_BLOCK - head_cols
    sc_block = 5
    gate_block = 8 * COL_BLOCK // d
    overlap_t = _overlap_t(n_sel, nrow)
    chunk = CMP_STRIDE * dh

    for l in range(depth):
        w_packed = jnp.concatenate(
            [w_in[l, :, :head_cols], jnp.zeros((d, pad_cols), F32), w_in[l, :, head_cols:]], axis=1).astype(BF16)
        proj = _inproj(x.reshape(n, d), mix_norm[l][None, :], w_packed)
        proj3 = proj.reshape(b, s, -1)

        y_rnn = _rnn_branch(
            proj3, rnn_conv_w[l], rnn_conv_b[l][None, :],
            _block_diag(rg_wa[l]).astype(BF16), rg_ba[l].reshape(1, rnn_w),
            _block_diag(rg_wx[l]).astype(BF16), rg_bx[l].reshape(1, rnn_w), rg_lambda[l][None, :])

        q5 = proj3[:, :, 2 * COL_BLOCK:3 * COL_BLOCK].reshape(b, s // Q_BLOCK, Q_BLOCK, kvh, NSA_GROUP, dh)
        q5 = q5.transpose(0, 3, 1, 4, 2, 5).reshape(b, kvh, s // Q_BLOCK, NSA_GROUP * Q_BLOCK, dh)
        kv = proj3[:, :, 3 * COL_BLOCK:3 * COL_BLOCK + 6 * kv_w].reshape(b, s, 6, kvh, dh)
        kv = kv.transpose(2, 0, 3, 1, 4)
        kc_flat = kv[0].reshape(b, kvh, nrow, chunk)
        vc_flat = kv[1].reshape(b, kvh, nrow, chunk)
        pe_flat = cmp_pe[l].reshape(2, 2, chunk)
        k_cmp, v_cmp, k_sel, k_win = _kvprep(
            kc_flat, vc_flat, kv[2], kv[4], pe_flat, cmp_w1[l].astype(BF16), cmp_b1[l][:, None, :],
            cmp_w2[l].astype(BF16), cmp_b2[l][:, None, :], k_norm[l])
        g0 = 3 * COL_BLOCK + 6 * kv_w
        gates4 = proj3[:, :, g0:g0 + n_gate].reshape(b, s, kvh, NSA_GROUP * N_NSA_BRANCHES).transpose(0, 2, 1, 3)
        y_nsa = _nsa(q5, k_cmp, v_cmp, k_sel, kv[3], k_win, kv[5], gates4, q_norm[l][None, :], overlap_t, n_cmp)

        x = _merge(x, proj3, y_rnn, y_nsa, sc_conv_w[l], w_rnn_out[l].astype(BF16), w_nsa_out[l].astype(BF16),
                   w_sc_out[l].astype(BF16), w_out[l].astype(BF16), sc_block, gate_block)

        w_router = jnp.concatenate(
            [router_expert_w[l], router_group_w[l], jnp.zeros((d, LANES - n_experts - n_groups), F32)],
            axis=1).astype(BF16)
        b_router = jnp.concatenate(
            [router_expert_b[l], router_group_b[l], jnp.zeros((LANES - n_experts - n_groups,), F32)])[None, :]
        x = _moe(x.reshape(n, d), ffn_norm[l][None, :], w_router, b_router, exp_w_gate[l].astype(BF16),
                 exp_w_up[l].astype(BF16), exp_w_down[l].astype(BF16), n_groups).reshape(b, s, d)
    return x
```

```python
import functools

import jax
import jax.numpy as jnp
from jax import lax
from jax.experimental import pallas as pl
from jax.experimental.pallas import tpu as pltpu

F32 = jnp.float32
BF16 = jnp.bfloat16

RMS_EPS = 1e-6
NEG_INF = -1e30
FORCE_SCORE = 1e6
RG_C = 8.0
HEAD_DIM = 64
NSA_GROUP = 4
N_NSA_BRANCHES = 3
CMP_LEN = 32
CMP_STRIDE = 16
SEL_LEN = 64
SEL_TOPK = 16
WINDOW = 512
Q_BLOCK = 256
EXPERTS_PER_GROUP = 8
TOPK_IN_GROUP = 2

LANES = 128
SUBLANES = 8
COL_BLOCK = 512
VMEM_LIMIT = 56 * 1024 * 1024
LOG2E = 1.4426950408889634


def _cparams(*sem):
    return pltpu.CompilerParams(dimension_semantics=sem, vmem_limit_bytes=VMEM_LIMIT)


def _rms(x, gain):
    return x * lax.rsqrt(jnp.mean(x * x, axis=-1, keepdims=True) + RMS_EPS) * gain


def _gelu_tanh(x):
    return 0.5 * x * (1.0 + jnp.tanh(0.7978845608028654 * (x + 0.044715 * x * x * x)))


def _dot(a, b):
    return jnp.dot(a, b, preferred_element_type=F32)


def _inproj_kernel(x_ref, g_ref, w_ref, o_ref):
    xn = _rms(x_ref[...], g_ref[...]).astype(BF16)
    for c in range(w_ref.shape[1] // COL_BLOCK):
        sl = slice(c * COL_BLOCK, (c + 1) * COL_BLOCK)
        o_ref[:, sl] = _dot(xn, w_ref[:, sl]).astype(BF16)


def _inproj(x2, gain, w_packed, tm=512):
    n, d = x2.shape
    wp = w_packed.shape[1]
    return pl.pallas_call(
        _inproj_kernel,
        grid=(n // tm,),
        in_specs=[
            pl.BlockSpec((tm, d), lambda i: (i, 0)),
            pl.BlockSpec((1, d), lambda i: (0, 0)),
            pl.BlockSpec((d, wp), lambda i: (0, 0), pipeline_mode=pl.Buffered(1)),
        ],
        out_specs=pl.BlockSpec((tm, wp), lambda i: (i, 0)),
        out_shape=jax.ShapeDtypeStruct((n, wp), BF16),
        compiler_params=_cparams("parallel"),
        name="inproj",
    )(x2, gain, w_packed)


def _rnn_kernel(x_ref, xp_ref, y_ref, cw_ref, cb_ref, wa_ref, ba_ref, wx_ref, bx_ref, lam_ref,
                o_ref, h_scr):
    ti = pl.program_id(1)
    ts = x_ref.shape[0]
    kw = cw_ref.shape[0]

    @pl.when(ti == 0)
    def _():
        h_scr[...] = jnp.zeros_like(h_scr)

    x = x_ref[...].astype(F32)
    halo = jnp.where(ti == 0, 0.0, xp_ref[...].astype(F32))
    xe = jnp.concatenate([halo, x], axis=0)
    u = cb_ref[...]
    for k in range(kw):
        off = SUBLANES - (kw - 1) + k
        u = u + cw_ref[k:k + 1, :] * xe[off:off + ts, :]
    ub = u.astype(BF16)
    r = jax.nn.sigmoid(_dot(ub, wa_ref[...]) + ba_ref[...])
    gi = jax.nn.sigmoid(_dot(ub, wx_ref[...]) + bx_ref[...])
    lam = lam_ref[...]
    softplus_neg = jnp.maximum(-lam, 0.0) + jnp.log(1.0 + jnp.exp(-jnp.abs(lam)))
    log_a = -RG_C * r * softplus_neg
    a = jnp.exp(log_a)
    b = jnp.sqrt(1.0 - jnp.exp(2.0 * log_a)) * (gi * u)

    row = lax.broadcasted_iota(jnp.int32, a.shape, 0)
    d = 1
    while d < ts:
        keep = row >= d
        a_sh = pltpu.roll(a, d, axis=0)
        b_sh = pltpu.roll(b, d, axis=0)
        b = jnp.where(keep, a * b_sh + b, b)
        a = jnp.where(keep, a * a_sh, a)
        d *= 2
    h = a * h_scr[...] + b
    h_scr[...] = h[ts - 1:ts, :]
    o_ref[...] = (_gelu_tanh(y_ref[...].astype(F32)) * h).astype(BF16)


def _rnn_branch(proj3, conv_w, conv_b, wa_bd, ba, wx_bd, bx, lam, ts=512):
    b, s, _ = proj3.shape
    w = conv_w.shape[1]
    nb = w // COL_BLOCK
    rows_per = ts // SUBLANES
    const = lambda shape: pl.BlockSpec(shape, lambda bi, ti: (0, 0))
    return pl.pallas_call(
        _rnn_kernel,
        grid=(b, s // ts),
        in_specs=[
            pl.BlockSpec((None, ts, w), lambda bi, ti: (bi, ti, 0)),
            pl.BlockSpec((None, SUBLANES, w), lambda bi, ti: (bi, jnp.maximum(ti * rows_per - 1, 0), 0)),
            pl.BlockSpec((None, ts, w), lambda bi, ti: (bi, ti, nb)),
            const(conv_w.shape), const((1, w)), const((w, w)), const((1, w)), const((w, w)), const((1, w)),
            const((1, w)),
        ],
        out_specs=pl.BlockSpec((None, ts, w), lambda bi, ti: (bi, ti, 0)),
        out_shape=jax.ShapeDtypeStruct((b, s, w), BF16),
        scratch_shapes=[pltpu.VMEM((1, w), F32)],
        compiler_params=_cparams("parallel", "arbitrary"),
        name="rnn_branch",
    )(proj3, proj3, proj3, conv_w, conv_b, wa_bd, ba, wx_bd, bx, lam)


def _kvprep_kernel(kcf_ref, vcf_ref, ks_ref, kw_ref, pe_ref, w1_ref, b1_ref, w2_ref, b2_ref, kn_ref,
                   kc_o, vc_o, ks_o, kw_o):
    half = kcf_ref.shape[1]
    nrow = kcf_ref.shape[0]

    def compress(flat_ref, i):
        xf = flat_ref[...].astype(F32)
        xa = (xf + pe_ref[i, 0:1, :]).astype(BF16)
        xb = (xf + pe_ref[i, 1:2, :]).astype(BF16)
        first = _dot(xa, w1_ref[i, 0:half, :])
        second = _dot(xb, w1_ref[i, half:2 * half, :])
        hid = first + pltpu.roll(second, nrow - 1, axis=0) + b1_ref[i]
        hid = _gelu_tanh(hid).astype(BF16)
        return _dot(hid, w2_ref[i]) + b2_ref[i]

    kc_o[...] = _rms(compress(kcf_ref, 0), kn_ref[0:1, :]).astype(BF16)
    vc_o[...] = compress(vcf_ref, 1).astype(BF16)
    ks_o[...] = _rms(ks_ref[...].astype(F32), kn_ref[1:2, :]).astype(BF16)
    kw_o[...] = _rms(kw_ref[...].astype(F32), kn_ref[2:3, :]).astype(BF16)


def _kvprep(kc_flat, vc_flat, ks, kw, pe_flat, w1, b1, w2, b2, k_norm):
    b, g, nrow, half = kc_flat.shape
    s = ks.shape[2]
    dh = ks.shape[3]
    bg = lambda shape: pl.BlockSpec((None, None) + shape, lambda bi, gi: (bi, gi, 0, 0))
    full = lambda a: pl.BlockSpec(a.shape, lambda bi, gi: (0,) * a.ndim)
    return pl.pallas_call(
        _kvprep_kernel,
        grid=(b, g),
        in_specs=[bg((nrow, half)), bg((nrow, half)), bg((s, dh)), bg((s, dh)),
                  full(pe_flat), full(w1), full(b1), full(w2), full(b2), full(k_norm)],
        out_specs=[bg((nrow, dh)), bg((nrow, dh)), bg((s, dh)), bg((s, dh))],
        out_shape=[jax.ShapeDtypeStruct((b, g, nrow, dh), BF16), jax.ShapeDtypeStruct((b, g, nrow, dh), BF16),
                   jax.ShapeDtypeStruct((b, g, s, dh), BF16), jax.ShapeDtypeStruct((b, g, s, dh), BF16)],
        compiler_params=_cparams("parallel", "parallel"),
        name="kvprep",
    )(kc_flat, vc_flat, ks, kw, pe_flat, w1, b1, w2, b2, k_norm)


def _nsa_kernel(qt_ref, kc_ref, vct_ref, ks_ref, vst_ref, kw_ref, vwt_ref, gt_ref, qn_ref, ov_ref, cb_ref, bb_ref,
                o_ref, sel_scr, *, key_tile):
    c = pl.program_id(2)
    tq = Q_BLOCK
    t0 = c * tq
    n_sel = ov_ref.shape[0]
    ncp = kc_ref.shape[0]
    top_k = min(SEL_TOPK, n_sel)
    lanes = NSA_GROUP * tq
    blocks_per_tile = key_tile // SEL_LEN

    qf = qt_ref[...].astype(F32)
    qn = qf * lax.rsqrt(jnp.mean(qf * qf, axis=0, keepdims=True) + RMS_EPS) * qn_ref[...]
    qs = (qn * (HEAD_DIM ** -0.5 * LOG2E)).astype(BF16)
    t_lane = t0 + (lax.broadcasted_iota(jnp.int32, (1, lanes), 1) & (tq - 1))

    n_tiles = (t0 + tq + key_tile - 1) // key_tile
    n_full = n_tiles - 1

    def scores(kt):
        k0 = pl.multiple_of(kt * key_tile, key_tile)
        return _dot(ks_ref[pl.ds(k0, key_tile), :], qs)

    wl = WINDOW + tq
    w0 = pl.multiple_of(jnp.maximum(t0 - WINDOW, 0), tq)
    s_c = _dot(kc_ref[...], qs)
    s_w = _dot(kw_ref[pl.ds(w0, wl), :], qs)
    s_l = scores(n_full)

    off_c = pl.multiple_of(cb_ref.shape[0] - ncp - t0 // CMP_STRIDE, SUBLANES)
    s_c = s_c + cb_ref[pl.ds(off_c, ncp), :]
    e_c = jnp.exp2(s_c - jnp.max(s_c, axis=0, keepdims=True))
    inv_c = 1.0 / jnp.sum(e_c, axis=0, keepdims=True)
    p_c = e_c * jnp.where(t_lane >= CMP_LEN - 1, inv_c, 0.0)
    o_c = _dot(vct_ref[...], p_c.astype(BF16))

    j_i = lax.broadcasted_iota(jnp.int32, (n_sel, tq), 0)
    cur = (t0 + lax.broadcasted_iota(jnp.int32, (n_sel, tq), 1)) // SEL_LEN
    ps = p_c[:, 0:tq]
    for h in range(1, NSA_GROUP):
        ps = ps + p_c[:, h * tq:(h + 1) * tq]
    ps_hi = ps.astype(BF16)
    ps_lo = (ps - ps_hi.astype(F32)).astype(BF16)
    imp = _dot(ov_ref[...], ps_hi) + _dot(ov_ref[...], ps_lo)
    forced = (j_i == 0) | (j_i == cur) | (j_i == cur - 1)
    imp = jnp.where(forced, FORCE_SCORE, imp)
    imp = jnp.where(j_i <= cur, imp, -1.0)
    slabs = [imp[v * SUBLANES:(v + 1) * SUBLANES, :] for v in range(n_sel // SUBLANES)]
    j_slab = lax.broadcasted_iota(jnp.int32, (SUBLANES, tq), 0)
    cnts = [jnp.zeros((SUBLANES, tq), jnp.int32) for _ in slabs]
    for jp in range(n_sel):
        r = imp[jp:jp + 1, :]
        for v, slab in enumerate(slabs):
            lo = v * SUBLANES
            if lo > jp:
                before = r >= slab
            elif lo + SUBLANES - 1 <= jp:
                before = r > slab
            else:
                before = (r > slab) | ((r == slab) & (j_slab + lo > jp))
            cnts[v] = cnts[v] + before.astype(jnp.int32)
    cnt = jnp.concatenate(cnts, axis=0)
    sel_bias = jnp.where((cnt < top_k) & (imp >= 0.0), 0.0, NEG_INF)
    sel_scr[...] = jnp.concatenate([sel_bias] * NSA_GROUP, axis=1)

    off_w = pl.multiple_of(WINDOW - (t0 - w0), tq)
    s_w = s_w + bb_ref[pl.ds(off_w, wl), :]
    e_w = jnp.exp2(s_w - jnp.max(s_w, axis=0, keepdims=True))
    inv_w = 1.0 / jnp.sum(e_w, axis=0, keepdims=True)
    o_w = _dot(vwt_ref[:, pl.ds(w0, wl)], e_w.astype(BF16)) * inv_w

    def block_bias(kt, valid=None):
        rows = []
        for jb in range(blocks_per_tile):
            row = sel_scr[pl.ds(kt * blocks_per_tile + jb, 1), :]
            if valid is not None:
                row = jnp.where(valid, row, NEG_INF)
            rows.append(jnp.broadcast_to(row, (SEL_LEN, lanes)))
        return jnp.concatenate(rows, axis=0)

    def update(carry, kt, s, bias):
        m_o, l_o, a_o = carry
        k0 = pl.multiple_of(kt * key_tile, key_tile)
        s = s + bias
        m_n = jnp.maximum(m_o, jnp.max(s, axis=0, keepdims=True))
        alpha = jnp.exp2(m_o - m_n)
        p = jnp.exp2(s - m_n)
        l_n = alpha * l_o + jnp.sum(p, axis=0, keepdims=True)
        a_n = alpha * a_o + _dot(vst_ref[:, pl.ds(k0, key_tile)], p.astype(BF16))
        return m_n, l_n, a_n

    def init():
        return (jnp.full((1, lanes), NEG_INF, F32), jnp.zeros((1, lanes), F32),
                jnp.zeros((HEAD_DIM, lanes), F32))

    off_l = pl.multiple_of(WINDOW - (t0 - n_full * key_tile), tq)
    c_first = update(init(), n_full, s_l, block_bias(n_full) + bb_ref[pl.ds(off_l, key_tile), :])

    def pair_step(i, carry):
        c_a, c_b = carry
        s_a = scores(2 * i)
        s_b = scores(2 * i + 1)
        c_a = update(c_a, 2 * i, s_a, block_bias(2 * i))
        c_b = update(c_b, 2 * i + 1, s_b, block_bias(2 * i + 1, 2 * i + 1 < n_full))
        return c_a, c_b

    (m_a, l_a, a_a), (m_b, l_b, a_b) = lax.fori_loop(0, (n_full + 1) // 2, pair_step, (c_first, init()))
    m_ab = jnp.maximum(m_a, m_b)
    w_a = jnp.exp2(m_a - m_ab)
    w_b = jnp.exp2(m_b - m_ab)
    o_s = (w_a * a_a + w_b * a_b) * (1.0 / (w_a * l_a + w_b * l_b))

    gates = jax.nn.sigmoid(gt_ref[...].astype(F32))
    for h in range(NSA_GROUP):
        hs = slice(h * tq, (h + 1) * tq)
        gb = h * N_NSA_BRANCHES
        o = (gates[gb:gb + 1, :] * o_c[:, hs] + gates[gb + 1:gb + 2, :] * o_s[:, hs]
             + gates[gb + 2:gb + 3, :] * o_w[:, hs])
        o_ref[:, h * HEAD_DIM:(h + 1) * HEAD_DIM] = o.T.astype(BF16)


def _band_bias(n_rows, lanes):
    i = jnp.arange(n_rows)[:, None]
    tl = (jnp.arange(lanes) % Q_BLOCK)[None, :]
    return jnp.where((tl < i) & (i <= tl + WINDOW), 0.0, NEG_INF).astype(F32)


def _cmp_bias(n_lead, ncp, lanes):
    i = jnp.arange(n_lead + ncp)[:, None]
    tl = (jnp.arange(lanes) % Q_BLOCK)[None, :]
    return jnp.where(CMP_STRIDE * (i - n_lead) + CMP_LEN - 1 <= tl, 0.0, NEG_INF).astype(F32)


def _nsa(qt5, k_cmp, v_cmp_t, k_sel, v_sel_t, k_win, v_win_t, gates_t, q_norm_col, overlap_t, key_tile=512):
    b, g, nchunk, dh, lanes = qt5.shape
    s = k_sel.shape[2]
    ncp = k_cmp.shape[2]
    ng = gates_t.shape[2]
    n_sel = overlap_t.shape[0]
    assert key_tile <= WINDOW and key_tile % Q_BLOCK == 0 and s % key_tile == 0
    cmp_bias = _cmp_bias((s - Q_BLOCK) // CMP_STRIDE, ncp, lanes)
    band_bias = _band_bias(2 * WINDOW + Q_BLOCK, lanes)
    bg = lambda shape: pl.BlockSpec((None, None) + shape, lambda bi, gi, ci: (bi, gi, 0, 0))
    const = lambda a: pl.BlockSpec(a.shape, lambda bi, gi, ci: (0, 0), pipeline_mode=pl.Buffered(1))
    return pl.pallas_call(
        functools.partial(_nsa_kernel, key_tile=key_tile),
        grid=(b, g, nchunk),
        in_specs=[
            pl.BlockSpec((None, None, None, dh, lanes), lambda bi, gi, ci: (bi, gi, ci, 0, 0)),
            bg((ncp, dh)), bg((dh, ncp)), bg((s, dh)), bg((dh, s)), bg((s, dh)), bg((dh, s)),
            pl.BlockSpec((None, None, ng, Q_BLOCK), lambda bi, gi, ci: (bi, gi, 0, ci)),
            pl.BlockSpec(q_norm_col.shape, lambda bi, gi, ci: (0, 0)),
            pl.BlockSpec(overlap_t.shape, lambda bi, gi, ci: (0, 0)),
            const(cmp_bias), const(band_bias),
        ],
        out_specs=pl.BlockSpec((None, Q_BLOCK, NSA_GROUP * dh), lambda bi, gi, ci: (bi, ci, gi)),
        out_shape=jax.ShapeDtypeStruct((b, s, g * NSA_GROUP * dh), BF16),
        scratch_shapes=[pltpu.VMEM((n_sel, lanes), F32)],
        compiler_params=_cparams("parallel", "parallel", "arbitrary"),
        name="nsa",
    )(qt5, k_cmp, v_cmp_t, k_sel, v_sel_t, k_win, v_win_t, gates_t, q_norm_col, overlap_t, cmp_bias, band_bias)


def _merge_kernel(x_ref, yr_ref, yn_ref, sb_ref, sc_ref, sh_ref, scp_ref, shp_ref, g0_ref, g1_ref, g2_ref,
                  cw_ref, wr_ref, wn_ref, ws_ref, wo_ref, o_ref):
    ti = pl.program_id(1)
    tm = x_ref.shape[0]
    kw = cw_ref.shape[0]
    prod = sc_ref[...].astype(F32) * sh_ref[...].astype(F32)
    halo = jnp.where(ti == 0, 0.0, scp_ref[...].astype(F32) * shp_ref[...].astype(F32))
    pe = jnp.concatenate([halo, prod], axis=0)
    conv = jnp.zeros_like(prod)
    for k in range(kw):
        off = SUBLANES - (kw - 1) + k
        conv = conv + cw_ref[k:k + 1, :] * pe[off:off + tm, :]
    y_sc = (sb_ref[...].astype(F32) * conv).astype(BF16)
    merged = (jax.nn.sigmoid(g0_ref[...].astype(F32)) * _dot(yr_ref[...], wr_ref[...])
              + jax.nn.sigmoid(g1_ref[...].astype(F32)) * _dot(yn_ref[...], wn_ref[...])
              + jax.nn.sigmoid(g2_ref[...].astype(F32)) * _dot(y_sc, ws_ref[...]))
    o_ref[...] = x_ref[...] + _dot(merged.astype(BF16), wo_ref[...])


def _merge(x3, proj3, y_rnn, y_nsa, sc_conv_w, w_rnn_out, w_nsa_out, w_sc_out, w_out, sc_block, gate_block, tm=256):
    b, s, d = x3.shape
    w = y_rnn.shape[2]
    rows_per = tm // SUBLANES
    tile = lambda width, col: pl.BlockSpec((None, tm, width), lambda bi, ti: (bi, ti, col))
    prev = lambda col: pl.BlockSpec((None, SUBLANES, w),
                                    lambda bi, ti: (bi, jnp.maximum(ti * rows_per - 1, 0), col))
    const = lambda a: pl.BlockSpec(a.shape, lambda bi, ti: (0, 0))
    return pl.pallas_call(
        _merge_kernel,
        grid=(b, s // tm),
        in_specs=[
            tile(d, 0), tile(w, 0), tile(w, 0),
            tile(w, sc_block), tile(w, sc_block + 1), tile(w, sc_block + 2),
            prev(sc_block + 1), prev(sc_block + 2),
            tile(d, gate_block), tile(d, gate_block + 1), tile(d, gate_block + 2),
            const(sc_conv_w), const(w_rnn_out), const(w_nsa_out), const(w_sc_out), const(w_out),
        ],
        out_specs=tile(d, 0),
        out_shape=jax.ShapeDtypeStruct((b, s, d), F32),
        compiler_params=_cparams("parallel", "parallel"),
        name="merge",
    )(x3, y_rnn, y_nsa, proj3, proj3, proj3, proj3, proj3, proj3, proj3, proj3,
      sc_conv_w, w_rnn_out, w_nsa_out, w_sc_out, w_out)


def _route(logits, n_experts, n_groups):
    lane_i = lax.broadcasted_iota(jnp.int32, logits.shape, 1)
    lane = lane_i.astype(F32)
    big = float(LANES)
    is_g = (lane_i >= n_experts) & (lane_i < n_experts + n_groups)
    gl = jnp.where(is_g, logits, NEG_INF)
    gmax = jnp.max(gl, axis=-1, keepdims=True)
    grp = jnp.min(jnp.where(is_g & (gl == gmax), lane - n_experts, big), axis=-1, keepdims=True)
    p_grp = 1.0 / jnp.sum(jnp.where(is_g, jnp.exp(gl - gmax), 0.0), axis=-1, keepdims=True)
    in_g = (lane_i < n_experts) & ((lane_i // EXPERTS_PER_GROUP).astype(F32) == grp)
    el = jnp.where(in_g, logits, NEG_INF)
    ee = jnp.where(in_g, jnp.exp(el - jnp.max(el, axis=-1, keepdims=True)), 0.0)
    ep = ee / jnp.sum(ee, axis=-1, keepdims=True)
    top1 = jnp.max(jnp.where(in_g, ep, -1.0), axis=-1, keepdims=True)
    i1 = jnp.min(jnp.where(in_g & (ep == top1), lane, big), axis=-1, keepdims=True)
    rest = jnp.where(in_g & (lane != i1), ep, -1.0)
    top2 = jnp.max(rest, axis=-1, keepdims=True)
    i2 = jnp.min(jnp.where(rest == top2, lane, big), axis=-1, keepdims=True)
    denom = top1 + top2
    return (jnp.where(lane == i1, p_grp * top1 / denom, 0.0)
            + jnp.where(lane == i2, p_grp * top2 / denom, 0.0))


def _moe_kernel(x_ref, g_ref, wr_ref, br_ref, wg_ref, wu_ref, wd_ref, o_ref, xn_scr, cw_scr, acc_scr,
                *, n_experts, n_groups):
    e = pl.program_id(1)

    @pl.when(e == 0)
    def _():
        xn = _rms(x_ref[...], g_ref[...]).astype(BF16)
        xn_scr[...] = xn
        cw_scr[...] = _route(_dot(xn, wr_ref[...]) + br_ref[...], n_experts, n_groups)
        acc_scr[...] = jnp.zeros_like(acc_scr)

    xn = xn_scr[...]
    lane = lax.broadcasted_iota(jnp.int32, cw_scr.shape, 1)
    cw = jnp.sum(jnp.where(lane == e, cw_scr[...], 0.0), axis=-1, keepdims=True)
    h = jax.nn.silu(_dot(xn, wg_ref[...])) * _dot(xn, wu_ref[...]) * cw
    acc_scr[...] += _dot(h.astype(BF16), wd_ref[...])

    @pl.when(e == n_experts - 1)
    def _():
        o_ref[...] = x_ref[...] + acc_scr[...]


def _moe(x2, gain, w_router, b_router, w_gate, w_up, w_down, n_groups, tm=512):
    n, d = x2.shape
    n_experts, _, f = w_gate.shape
    const = lambda a: pl.BlockSpec(a.shape, lambda i, e: (0, 0))
    return pl.pallas_call(
        functools.partial(_moe_kernel, n_experts=n_experts, n_groups=n_groups),
        grid=(n // tm, n_experts),
        in_specs=[
            pl.BlockSpec((tm, d), lambda i, e: (i, 0)),
            const(gain), const(w_router), const(b_router),
            pl.BlockSpec((None, d, f), lambda i, e: (e, 0, 0)),
            pl.BlockSpec((None, d, f), lambda i, e: (e, 0, 0)),
            pl.BlockSpec((None, f, d), lambda i, e: (e, 0, 0)),
        ],
        out_specs=pl.BlockSpec((tm, d), lambda i, e: (i, 0)),
        out_shape=jax.ShapeDtypeStruct((n, d), F32),
        scratch_shapes=[pltpu.VMEM((tm, d), BF16), pltpu.VMEM((tm, LANES), F32), pltpu.VMEM((tm, d), F32)],
        compiler_params=_cparams("parallel", "arbitrary"),
        name="moe",
    )(x2, gain, w_router, b_router, w_gate, w_up, w_down)


def _block_diag(w):
    nb, bs, _ = w.shape
    eye = jnp.eye(nb, dtype=w.dtype)
    return (eye[:, None, :, None] * w[:, :, None, :]).reshape(nb * bs, nb * bs)


def _overlap_t(n_sel, ncp):
    n = jnp.arange(ncp)
    j = jnp.arange(n_sel)
    c0 = n * CMP_STRIDE
    s0 = j * SEL_LEN
    ov = jnp.clip(jnp.minimum(c0[None, :] + CMP_LEN, s0[:, None] + SEL_LEN)
                  - jnp.maximum(c0[None, :], s0[:, None]), 0).astype(F32) / CMP_LEN
    return ov.astype(BF16)


def kernel(x, mix_norm, w_in, rnn_conv_w, rnn_conv_b, rg_wa, rg_ba, rg_wx, rg_bx, rg_lambda, cmp_pe, cmp_w1, cmp_b1, cmp_w2, cmp_b2, q_norm, k_norm, sc_conv_w, w_rnn_out, w_nsa_out, w_sc_out, w_out, ffn_norm, router_group_w, router_group_b, router_expert_w, router_expert_b, exp_w_gate, exp_w_up, exp_w_down):
    b, s, d = x.shape
    depth = w_in.shape[0]
    rnn_w = rnn_conv_w.shape[2]
    sc_w = sc_conv_w.shape[2]
    nsa_w = w_nsa_out.shape[1]
    dh = q_norm.shape[1]
    n_heads = nsa_w // dh
    kvh = n_heads // NSA_GROUP
    kv_w = kvh * dh
    n_groups = router_group_w.shape[2]
    n_experts = router_expert_w.shape[2]
    n = b * s
    assert rnn_w == COL_BLOCK and sc_w == COL_BLOCK and nsa_w == COL_BLOCK and dh == HEAD_DIM
    assert 4 * kv_w == COL_BLOCK and d % COL_BLOCK == 0
    assert s % COL_BLOCK == 0 and s >= WINDOW + Q_BLOCK and s % SEL_LEN == 0
    assert n_experts + n_groups <= LANES and n_experts == n_groups * EXPERTS_PER_GROUP

    nrow = s // CMP_STRIDE
    n_sel = s // SEL_LEN
    n_gate = n_heads * N_NSA_BRANCHES
    head_cols = 3 * COL_BLOCK + 6 * kv_w + n_gate
    pad_cols = 5 * COL_BLOCK - head_cols
    sc_block = 5
    gate_block = 8 * COL_BLOCK // d
    overlap_t = _overlap_t(n_sel, nrow)
    chunk = CMP_STRIDE * dh

    for l in range(depth):
        w_packed = jnp.concatenate(
            [w_in[l, :, :head_cols], jnp.zeros((d, pad_cols), F32), w_in[l, :, head_cols:]], axis=1).astype(BF16)
        proj = _inproj(x.reshape(n, d), mix_norm[l][None, :], w_packed)
        proj3 = proj.reshape(b, s, -1)

        y_rnn = _rnn_branch(
            proj3, rnn_conv_w[l], rnn_conv_b[l][None, :],
            _block_diag(rg_wa[l]).astype(BF16), rg_ba[l].reshape(1, rnn_w),
            _block_diag(rg_wx[l]).astype(BF16), rg_bx[l].reshape(1, rnn_w), rg_lambda[l][None, :])

        qt5 = proj3[:, :, 2 * COL_BLOCK:3 * COL_BLOCK].reshape(b, s // Q_BLOCK, Q_BLOCK, kvh, NSA_GROUP, dh)
        qt5 = qt5.transpose(0, 3, 1, 5, 4, 2).reshape(b, kvh, s // Q_BLOCK, dh, NSA_GROUP * Q_BLOCK)
        kv = proj3[:, :, 3 * COL_BLOCK:3 * COL_BLOCK + 6 * kv_w].reshape(b, s, 6, kvh, dh)
        kv_rows = kv.transpose(2, 0, 3, 1, 4)
        kv_cols = kv.transpose(2, 0, 3, 4, 1)
        kc_flat = kv_rows[0].reshape(b, kvh, nrow, chunk)
        vc_flat = kv_rows[1].reshape(b, kvh, nrow, chunk)
        pe_flat = cmp_pe[l].reshape(2, 2, chunk)
        k_cmp, v_cmp, k_sel, k_win = _kvprep(
            kc_flat, vc_flat, kv_rows[2], kv_rows[4], pe_flat, cmp_w1[l].astype(BF16), cmp_b1[l][:, None, :],
            cmp_w2[l].astype(BF16), cmp_b2[l][:, None, :], k_norm[l])
        g0 = 3 * COL_BLOCK + 6 * kv_w
        gates_t = proj3[:, :, g0:g0 + n_gate].reshape(b, s, kvh, NSA_GROUP * N_NSA_BRANCHES).transpose(0, 2, 3, 1)
        y_nsa = _nsa(qt5, k_cmp, v_cmp.transpose(0, 1, 3, 2), k_sel, kv_cols[3], k_win, kv_cols[5], gates_t,
                     q_norm[l][:, None], overlap_t)

        x = _merge(x, proj3, y_rnn, y_nsa, sc_conv_w[l], w_rnn_out[l].astype(BF16), w_nsa_out[l].astype(BF16),
                   w_sc_out[l].astype(BF16), w_out[l].astype(BF16), sc_block, gate_block)

        w_router = jnp.concatenate(
            [router_expert_w[l], router_group_w[l], jnp.zeros((d, LANES - n_experts - n_groups), F32)],
            axis=1).astype(BF16)
        b_router = jnp.concatenate(
            [router_expert_b[l], router_group_b[l], jnp.zeros((LANES - n_experts - n_groups,), F32)])[None, :]
        x = _moe(x.reshape(n, d), ffn_norm[l][None, :], w_router, b_router, exp_w_gate[l].astype(BF16),
                 exp_w_up[l].astype(BF16), exp_w_down[l].astype(BF16), n_groups).reshape(b, s, d)
    return x
```

```python
import functools

import jax
import jax.numpy as jnp
from jax import lax
from jax.experimental import pallas as pl
from jax.experimental.pallas import tpu as pltpu

F32 = jnp.float32
BF16 = jnp.bfloat16

RMS_EPS = 1e-6
NEG_INF = -1e30
FORCE_SCORE = 1e6
RG_C = 8.0
HEAD_DIM = 64
NSA_GROUP = 4
N_NSA_BRANCHES = 3
CMP_LEN = 32
CMP_STRIDE = 16
SEL_LEN = 64
SEL_TOPK = 16
WINDOW = 512
Q_BLOCK = 256
EXPERTS_PER_GROUP = 8
TOPK_IN_GROUP = 2

LANES = 128
SUBLANES = 8
COL_BLOCK = 512
VMEM_LIMIT = 56 * 1024 * 1024
LOG2E = 1.4426950408889634


def _cparams(*sem):
    return pltpu.CompilerParams(dimension_semantics=sem, vmem_limit_bytes=VMEM_LIMIT)


def _rms(x, gain):
    return x * lax.rsqrt(jnp.mean(x * x, axis=-1, keepdims=True) + RMS_EPS) * gain


def _gelu_tanh(x):
    return 0.5 * x * (1.0 + jnp.tanh(0.7978845608028654 * (x + 0.044715 * x * x * x)))


def _dot(a, b):
    return jnp.dot(a, b, preferred_element_type=F32)


def _inproj_kernel(*refs, has_y):
    if has_y:
        x_ref, y_ref, g_ref, w_ref, o_ref, xo_ref = refs
        x = x_ref[...] + y_ref[...].astype(F32)
        xo_ref[...] = x
    else:
        x_ref, g_ref, w_ref, o_ref = refs
        x = x_ref[...]
    xn = _rms(x, g_ref[...]).astype(BF16)
    for c in range(w_ref.shape[1] // COL_BLOCK):
        sl = slice(c * COL_BLOCK, (c + 1) * COL_BLOCK)
        o_ref[:, sl] = _dot(xn, w_ref[:, sl]).astype(BF16)


def _inproj(x2, y2, gain, w_packed, tm=512):
    n, d = x2.shape
    wp = w_packed.shape[1]
    row = pl.BlockSpec((tm, d), lambda i: (i, 0))
    has_y = y2 is not None
    out_specs = [pl.BlockSpec((tm, wp), lambda i: (i, 0))] + ([row] if has_y else [])
    out_shape = [jax.ShapeDtypeStruct((n, wp), BF16)] + ([jax.ShapeDtypeStruct((n, d), F32)] if has_y else [])
    res = pl.pallas_call(
        functools.partial(_inproj_kernel, has_y=has_y),
        grid=(n // tm,),
        in_specs=[row] + ([row] if has_y else []) + [
            pl.BlockSpec((1, d), lambda i: (0, 0)),
            pl.BlockSpec((d, wp), lambda i: (0, 0), pipeline_mode=pl.Buffered(1)),
        ],
        out_specs=out_specs,
        out_shape=out_shape,
        compiler_params=_cparams("parallel"),
        name="inproj",
    )(*([x2, y2] if has_y else [x2]), gain, w_packed)
    return (res[0], res[1]) if has_y else (res[0], x2)


def _rnn_kernel(x_ref, xp_ref, y_ref, cw_ref, cb_ref, wa_ref, ba_ref, wx_ref, bx_ref, lam_ref,
                o_ref, h_scr):
    ti = pl.program_id(1)
    ts = x_ref.shape[0]
    kw = cw_ref.shape[0]

    @pl.when(ti == 0)
    def _():
        h_scr[...] = jnp.zeros_like(h_scr)

    x = x_ref[...].astype(F32)
    halo = jnp.where(ti == 0, 0.0, xp_ref[...].astype(F32))
    xe = jnp.concatenate([halo, x], axis=0)
    u = cb_ref[...]
    for k in range(kw):
        off = SUBLANES - (kw - 1) + k
        u = u + cw_ref[k:k + 1, :] * xe[off:off + ts, :]
    ub = u.astype(BF16)
    r = jax.nn.sigmoid(_dot(ub, wa_ref[...]) + ba_ref[...])
    gi = jax.nn.sigmoid(_dot(ub, wx_ref[...]) + bx_ref[...])
    lam = lam_ref[...]
    softplus_neg = jnp.maximum(-lam, 0.0) + jnp.log(1.0 + jnp.exp(-jnp.abs(lam)))
    log_a = -RG_C * r * softplus_neg
    a = jnp.exp(log_a)
    b = jnp.sqrt(1.0 - jnp.exp(2.0 * log_a)) * (gi * u)

    row = lax.broadcasted_iota(jnp.int32, a.shape, 0)
    d = 1
    while d < ts:
        keep = row >= d
        a_sh = pltpu.roll(a, d, axis=0)
        b_sh = pltpu.roll(b, d, axis=0)
        b = jnp.where(keep, a * b_sh + b, b)
        a = jnp.where(keep, a * a_sh, a)
        d *= 2
    h = a * h_scr[...] + b
    h_scr[...] = h[ts - 1:ts, :]
    o_ref[...] = (_gelu_tanh(y_ref[...].astype(F32)) * h).astype(BF16)


def _rnn_branch(proj3, conv_w, conv_b, wa_bd, ba, wx_bd, bx, lam, ts=512):
    b, s, _ = proj3.shape
    w = conv_w.shape[1]
    nb = w // COL_BLOCK
    rows_per = ts // SUBLANES
    const = lambda shape: pl.BlockSpec(shape, lambda bi, ti: (0, 0))
    return pl.pallas_call(
        _rnn_kernel,
        grid=(b, s // ts),
        in_specs=[
            pl.BlockSpec((None, ts, w), lambda bi, ti: (bi, ti, 0)),
            pl.BlockSpec((None, SUBLANES, w), lambda bi, ti: (bi, jnp.maximum(ti * rows_per - 1, 0), 0)),
            pl.BlockSpec((None, ts, w), lambda bi, ti: (bi, ti, nb)),
            const(conv_w.shape), const((1, w)), const((w, w)), const((1, w)), const((w, w)), const((1, w)),
            const((1, w)),
        ],
        out_specs=pl.BlockSpec((None, ts, w), lambda bi, ti: (bi, ti, 0)),
        out_shape=jax.ShapeDtypeStruct((b, s, w), BF16),
        scratch_shapes=[pltpu.VMEM((1, w), F32)],
        compiler_params=_cparams("parallel", "arbitrary"),
        name="rnn_branch",
    )(proj3, proj3, proj3, conv_w, conv_b, wa_bd, ba, wx_bd, bx, lam)


def _kvprep_kernel(kcf_ref, vcf_ref, ks_ref, kw_ref, pe_ref, w1_ref, b1_ref, w2_ref, b2_ref, kn_ref,
                   kc_o, vc_o, ks_o, kw_o):
    half = kcf_ref.shape[1]
    nrow = kcf_ref.shape[0]

    def compress(flat_ref, i):
        xf = flat_ref[...].astype(F32)
        xa = (xf + pe_ref[i, 0:1, :]).astype(BF16)
        xb = (xf + pe_ref[i, 1:2, :]).astype(BF16)
        first = _dot(xa, w1_ref[i, 0:half, :])
        second = _dot(xb, w1_ref[i, half:2 * half, :])
        hid = first + pltpu.roll(second, nrow - 1, axis=0) + b1_ref[i]
        hid = _gelu_tanh(hid).astype(BF16)
        return _dot(hid, w2_ref[i]) + b2_ref[i]

    kc_o[...] = _rms(compress(kcf_ref, 0), kn_ref[0:1, :]).astype(BF16)
    vc_o[...] = compress(vcf_ref, 1).astype(BF16)
    ks = _rms(ks_ref[...].astype(F32), kn_ref[1:2, :])
    row = lax.broadcasted_iota(jnp.int32, ks.shape, 0)
    col = lax.broadcasted_iota(jnp.int32, ks.shape, 1)
    block_hot = jnp.where((row // SEL_LEN) % SUBLANES == col, 1.0, 0.0)
    ks_o[...] = jnp.concatenate([ks, block_hot], axis=1).astype(BF16)
    kw_o[...] = _rms(kw_ref[...].astype(F32), kn_ref[2:3, :]).astype(BF16)


def _kvprep(kc_flat, vc_flat, ks, kw, pe_flat, w1, b1, w2, b2, k_norm):
    b, g, nrow, half = kc_flat.shape
    s = ks.shape[2]
    dh = ks.shape[3]
    bg = lambda shape: pl.BlockSpec((None, None) + shape, lambda bi, gi: (bi, gi, 0, 0))
    full = lambda a: pl.BlockSpec(a.shape, lambda bi, gi: (0,) * a.ndim)
    return pl.pallas_call(
        _kvprep_kernel,
        grid=(b, g),
        in_specs=[bg((nrow, half)), bg((nrow, half)), bg((s, dh)), bg((s, dh)),
                  full(pe_flat), full(w1), full(b1), full(w2), full(b2), full(k_norm)],
        out_specs=[bg((nrow, dh)), bg((nrow, dh)), bg((s, 2 * dh)), bg((s, dh))],
        out_shape=[jax.ShapeDtypeStruct((b, g, nrow, dh), BF16), jax.ShapeDtypeStruct((b, g, nrow, dh), BF16),
                   jax.ShapeDtypeStruct((b, g, s, 2 * dh), BF16), jax.ShapeDtypeStruct((b, g, s, dh), BF16)],
        compiler_params=_cparams("parallel", "parallel"),
        name="kvprep",
    )(kc_flat, vc_flat, ks, kw, pe_flat, w1, b1, w2, b2, k_norm)


def _nsa_kernel(qt_ref, kc_ref, vct_ref, ks_ref, vst_ref, kw_ref, vwt_ref, gt_ref, qn_ref, ov_ref, cb_ref, bb_ref,
                o_ref, sel_scr, *, key_tile):
    c = pl.program_id(2)
    tq = Q_BLOCK
    t0 = c * tq
    n_sel = ov_ref.shape[0]
    ncp = kc_ref.shape[0]
    top_k = min(SEL_TOPK, n_sel)
    lanes = NSA_GROUP * tq
    blocks_per_tile = key_tile // SEL_LEN

    qf = qt_ref[...].astype(F32)
    qn = qf * lax.rsqrt(jnp.mean(qf * qf, axis=0, keepdims=True) + RMS_EPS) * qn_ref[...]
    qs = (qn * (HEAD_DIM ** -0.5 * LOG2E)).astype(BF16)
    t_lane = t0 + (lax.broadcasted_iota(jnp.int32, (1, lanes), 1) & (tq - 1))

    n_tiles = (t0 + tq + key_tile - 1) // key_tile
    n_full = n_tiles - 1

    pad_rows = ks_ref.shape[1] - HEAD_DIM - 2 * SUBLANES
    assert blocks_per_tile == SUBLANES and pad_rows >= 0

    def scores(kt, bias_rows=None):
        k0 = pl.multiple_of(kt * key_tile, key_tile)
        if bias_rows is None:
            bias_rows = jnp.zeros((SUBLANES, lanes), F32)
        extra = jnp.concatenate([bias_rows, jnp.zeros((SUBLANES, lanes), F32)], axis=0).astype(BF16)
        q_aug = jnp.concatenate([qs, extra, jnp.zeros((pad_rows, lanes), BF16)], axis=0)
        return _dot(ks_ref[pl.ds(k0, key_tile), :], q_aug)

    wl = WINDOW + tq
    w0 = pl.multiple_of(jnp.maximum(t0 - WINDOW, 0), tq)
    s_c = _dot(kc_ref[...], qs)
    s_w = _dot(kw_ref[pl.ds(w0, wl), :], qs)
    s_l = scores(n_full)

    off_c = pl.multiple_of(cb_ref.shape[0] - ncp - t0 // CMP_STRIDE, SUBLANES)
    s_c = s_c + cb_ref[pl.ds(off_c, ncp), :]
    e_c = jnp.exp2(s_c - jnp.max(s_c, axis=0, keepdims=True))
    inv_c = 1.0 / jnp.sum(e_c, axis=0, keepdims=True)
    p_c = e_c * jnp.where(t_lane >= CMP_LEN - 1, inv_c, 0.0)
    o_c = _dot(vct_ref[...], p_c.astype(BF16))

    j_i = lax.broadcasted_iota(jnp.int32, (n_sel, tq), 0)
    cur = (t0 + lax.broadcasted_iota(jnp.int32, (n_sel, tq), 1)) // SEL_LEN
    ps = p_c[:, 0:tq]
    for h in range(1, NSA_GROUP):
        ps = ps + p_c[:, h * tq:(h + 1) * tq]
    ps_hi = ps.astype(BF16)
    ps_lo = (ps - ps_hi.astype(F32)).astype(BF16)
    imp = _dot(ov_ref[...], ps_hi) + _dot(ov_ref[...], ps_lo)
    forced = (j_i == 0) | (j_i == cur) | (j_i == cur - 1)
    imp = jnp.where(forced, FORCE_SCORE, imp)
    imp = jnp.where(j_i <= cur, imp, -1.0)
    slabs = [imp[v * SUBLANES:(v + 1) * SUBLANES, :] for v in range(n_sel // SUBLANES)]
    j_slab = lax.broadcasted_iota(jnp.int32, (SUBLANES, tq), 0)
    cnts = [jnp.zeros((SUBLANES, tq), jnp.int32) for _ in slabs]
    for jp in range(n_sel):
        r = imp[jp:jp + 1, :]
        for v, slab in enumerate(slabs):
            lo = v * SUBLANES
            if lo > jp:
                before = r >= slab
            elif lo + SUBLANES - 1 <= jp:
                before = r > slab
            else:
                before = (r > slab) | ((r == slab) & (j_slab + lo > jp))
            cnts[v] = cnts[v] + before.astype(jnp.int32)
    cnt = jnp.concatenate(cnts, axis=0)
    sel_bias = jnp.where((cnt < top_k) & (imp >= 0.0), 0.0, NEG_INF)
    sel_scr[...] = jnp.concatenate([sel_bias] * NSA_GROUP, axis=1)

    off_w = pl.multiple_of(WINDOW - (t0 - w0), tq)
    s_w = s_w + bb_ref[pl.ds(off_w, wl), :]
    e_w = jnp.exp2(s_w - jnp.max(s_w, axis=0, keepdims=True))
    a_w = _dot(vwt_ref[:, pl.ds(w0, wl)], e_w.astype(BF16))
    o_w = a_w[0:HEAD_DIM] * (1.0 / a_w[HEAD_DIM:HEAD_DIM + 1])

    def bias_rows(kt):
        return sel_scr[pl.ds(pl.multiple_of(kt * blocks_per_tile, SUBLANES), blocks_per_tile), :]

    def update(carry, kt, s):
        m_o, a_o = carry
        k0 = pl.multiple_of(kt * key_tile, key_tile)
        m_n = jnp.maximum(m_o, jnp.max(s, axis=0, keepdims=True))
        p = jnp.exp2(s - m_n).astype(BF16)
        a_n = jnp.exp2(m_o - m_n) * a_o + _dot(vst_ref[:, pl.ds(k0, key_tile)], p)
        return m_n, a_n

    def init():
        return jnp.full((1, lanes), NEG_INF, F32), jnp.zeros((vst_ref.shape[0], lanes), F32)

    off_l = pl.multiple_of(WINDOW - (t0 - n_full * key_tile), tq)
    last_rows = bias_rows(n_full)
    last_bias = jnp.concatenate(
        [jnp.broadcast_to(last_rows[jb:jb + 1, :], (SEL_LEN, lanes)) for jb in range(blocks_per_tile)], axis=0)
    c_first = update(init(), n_full, s_l + last_bias + bb_ref[pl.ds(off_l, key_tile), :])

    def pair_step(i, carry):
        c_a, c_b = carry
        s_a = scores(2 * i, bias_rows(2 * i))
        s_b = scores(2 * i + 1, jnp.where(2 * i + 1 < n_full, bias_rows(2 * i + 1), NEG_INF))
        return update(c_a, 2 * i, s_a), update(c_b, 2 * i + 1, s_b)

    (m_a, a_a), (m_b, a_b) = lax.fori_loop(0, (n_full + 1) // 2, pair_step, (c_first, init()))
    m_ab = jnp.maximum(m_a, m_b)
    a_s = jnp.exp2(m_a - m_ab) * a_a + jnp.exp2(m_b - m_ab) * a_b
    o_s = a_s[0:HEAD_DIM] * (1.0 / a_s[HEAD_DIM:HEAD_DIM + 1])

    gates = jax.nn.sigmoid(gt_ref[...].astype(F32))
    for h in range(NSA_GROUP):
        hs = slice(h * tq, (h + 1) * tq)
        gb = h * N_NSA_BRANCHES
        o = (gates[gb:gb + 1, :] * o_c[:, hs] + gates[gb + 1:gb + 2, :] * o_s[:, hs]
             + gates[gb + 2:gb + 3, :] * o_w[:, hs])
        o_ref[:, h * HEAD_DIM:(h + 1) * HEAD_DIM] = o.T.astype(BF16)


def _band_bias(n_rows, lanes):
    i = jnp.arange(n_rows)[:, None]
    tl = (jnp.arange(lanes) % Q_BLOCK)[None, :]
    return jnp.where((tl < i) & (i <= tl + WINDOW), 0.0, NEG_INF).astype(F32)


def _cmp_bias(n_lead, ncp, lanes):
    i = jnp.arange(n_lead + ncp)[:, None]
    tl = (jnp.arange(lanes) % Q_BLOCK)[None, :]
    return jnp.where(CMP_STRIDE * (i - n_lead) + CMP_LEN - 1 <= tl, 0.0, NEG_INF).astype(F32)


def _nsa(qt5, k_cmp, v_cmp_t, k_sel, v_sel_t, k_win, v_win_t, gates_t, q_norm_col, overlap_t, key_tile=512):
    b, g, nchunk, dh, lanes = qt5.shape
    s = k_sel.shape[2]
    ncp = k_cmp.shape[2]
    ng = gates_t.shape[2]
    n_sel = overlap_t.shape[0]
    assert key_tile <= WINDOW and key_tile % Q_BLOCK == 0 and s % key_tile == 0
    cmp_bias = _cmp_bias((s - Q_BLOCK) // CMP_STRIDE, ncp, lanes)
    band_bias = _band_bias(2 * WINDOW + Q_BLOCK, lanes)
    bg = lambda shape: pl.BlockSpec((None, None) + shape, lambda bi, gi, ci: (bi, gi, 0, 0))
    const = lambda a: pl.BlockSpec(a.shape, lambda bi, gi, ci: (0, 0), pipeline_mode=pl.Buffered(1))
    return pl.pallas_call(
        functools.partial(_nsa_kernel, key_tile=key_tile),
        grid=(b, g, nchunk),
        in_specs=[
            pl.BlockSpec((None, None, None, dh, lanes), lambda bi, gi, ci: (bi, gi, ci, 0, 0)),
            bg((ncp, dh)), bg((dh, ncp)), bg((s, k_sel.shape[3])), bg((v_sel_t.shape[2], s)), bg((s, dh)),
            bg((v_win_t.shape[2], s)),
            pl.BlockSpec((None, None, ng, Q_BLOCK), lambda bi, gi, ci: (bi, gi, 0, ci)),
            pl.BlockSpec(q_norm_col.shape, lambda bi, gi, ci: (0, 0)),
            pl.BlockSpec(overlap_t.shape, lambda bi, gi, ci: (0, 0)),
            const(cmp_bias), const(band_bias),
        ],
        out_specs=pl.BlockSpec((None, Q_BLOCK, NSA_GROUP * dh), lambda bi, gi, ci: (bi, ci, gi)),
        out_shape=jax.ShapeDtypeStruct((b, s, g * NSA_GROUP * dh), BF16),
        scratch_shapes=[pltpu.VMEM((n_sel, lanes), F32)],
        compiler_params=_cparams("parallel", "parallel", "arbitrary"),
        name="nsa",
    )(qt5, k_cmp, v_cmp_t, k_sel, v_sel_t, k_win, v_win_t, gates_t, q_norm_col, overlap_t, cmp_bias, band_bias)


def _merge_kernel(x_ref, yr_ref, yn_ref, sb_ref, sc_ref, sh_ref, scp_ref, shp_ref, g0_ref, g1_ref, g2_ref,
                  cw_ref, wr_ref, wn_ref, ws_ref, wo_ref, o_ref):
    ti = pl.program_id(1)
    tm = x_ref.shape[0]
    kw = cw_ref.shape[0]
    prod = sc_ref[...].astype(F32) * sh_ref[...].astype(F32)
    halo = jnp.where(ti == 0, 0.0, scp_ref[...].astype(F32) * shp_ref[...].astype(F32))
    pe = jnp.concatenate([halo, prod], axis=0)
    conv = jnp.zeros_like(prod)
    for k in range(kw):
        off = SUBLANES - (kw - 1) + k
        conv = conv + cw_ref[k:k + 1, :] * pe[off:off + tm, :]
    y_sc = (sb_ref[...].astype(F32) * conv).astype(BF16)
    merged = (jax.nn.sigmoid(g0_ref[...].astype(F32)) * _dot(yr_ref[...], wr_ref[...])
              + jax.nn.sigmoid(g1_ref[...].astype(F32)) * _dot(yn_ref[...], wn_ref[...])
              + jax.nn.sigmoid(g2_ref[...].astype(F32)) * _dot(y_sc, ws_ref[...]))
    o_ref[...] = x_ref[...] + _dot(merged.astype(BF16), wo_ref[...])


def _merge(x3, proj3, y_rnn, y_nsa, sc_conv_w, w_rnn_out, w_nsa_out, w_sc_out, w_out, sc_block, gate_block, tm=256):
    b, s, d = x3.shape
    w = y_rnn.shape[2]
    rows_per = tm // SUBLANES
    tile = lambda width, col: pl.BlockSpec((None, tm, width), lambda bi, ti: (bi, ti, col))
    prev = lambda col: pl.BlockSpec((None, SUBLANES, w),
                                    lambda bi, ti: (bi, jnp.maximum(ti * rows_per - 1, 0), col))
    const = lambda a: pl.BlockSpec(a.shape, lambda bi, ti: (0, 0))
    return pl.pallas_call(
        _merge_kernel,
        grid=(b, s // tm),
        in_specs=[
            tile(d, 0), tile(w, 0), tile(w, 0),
            tile(w, sc_block), tile(w, sc_block + 1), tile(w, sc_block + 2),
            prev(sc_block + 1), prev(sc_block + 2),
            tile(d, gate_block), tile(d, gate_block + 1), tile(d, gate_block + 2),
            const(sc_conv_w), const(w_rnn_out), const(w_nsa_out), const(w_sc_out), const(w_out),
        ],
        out_specs=tile(d, 0),
        out_shape=jax.ShapeDtypeStruct((b, s, d), F32),
        compiler_params=_cparams("parallel", "parallel"),
        name="merge",
    )(x3, y_rnn, y_nsa, proj3, proj3, proj3, proj3, proj3, proj3, proj3, proj3,
      sc_conv_w, w_rnn_out, w_nsa_out, w_sc_out, w_out)


MOE_BLOCK = 1024
MOE_ROW_TILE = 256
SLOT_IND = EXPERTS_PER_GROUP
SLOT_POS = EXPERTS_PER_GROUP + 1


def _route(logits, n_experts, n_groups):
    lane_i = lax.broadcasted_iota(jnp.int32, logits.shape, 1)
    lane = lane_i.astype(F32)
    big = float(LANES)
    is_g = (lane_i >= n_experts) & (lane_i < n_experts + n_groups)
    gl = jnp.where(is_g, logits, NEG_INF)
    gmax = jnp.max(gl, axis=-1, keepdims=True)
    grp = jnp.min(jnp.where(is_g & (gl == gmax), lane - n_experts, big), axis=-1, keepdims=True)
    p_grp = 1.0 / jnp.sum(jnp.where(is_g, jnp.exp(gl - gmax), 0.0), axis=-1, keepdims=True)
    in_g = (lane_i < n_experts) & ((lane_i // EXPERTS_PER_GROUP).astype(F32) == grp)
    el = jnp.where(in_g, logits, NEG_INF)
    ee = jnp.where(in_g, jnp.exp(el - jnp.max(el, axis=-1, keepdims=True)), 0.0)
    ep = ee / jnp.sum(ee, axis=-1, keepdims=True)
    top1 = jnp.max(jnp.where(in_g, ep, -1.0), axis=-1, keepdims=True)
    i1 = jnp.min(jnp.where(in_g & (ep == top1), lane, big), axis=-1, keepdims=True)
    rest = jnp.where(in_g & (lane != i1), ep, -1.0)
    top2 = jnp.max(rest, axis=-1, keepdims=True)
    i2 = jnp.min(jnp.where(rest == top2, lane, big), axis=-1, keepdims=True)
    denom = top1 + top2
    cw = (jnp.where(lane == i1, p_grp * top1 / denom, 0.0)
          + jnp.where(lane == i2, p_grp * top2 / denom, 0.0))
    return cw, grp


def _router_kernel(x_ref, g_ref, wr_ref, br_ref, tri_ref, xn_ref, slot_ref, info_t_ref, *, n_experts, n_groups):
    xn = _rms(x_ref[...], g_ref[...]).astype(BF16)
    xn_ref[...] = xn
    cw, grp = _route(_dot(xn, wr_ref[...]) + br_ref[...], n_experts, n_groups)
    lane_i = lax.broadcasted_iota(jnp.int32, cw.shape, 1)
    lane = lane_i.astype(F32)
    ind = jnp.where((lane == grp) & (lane_i < n_groups), 1.0, 0.0)
    pos = _dot(tri_ref[...], ind.astype(BF16))
    for g in range(n_groups):
        shift = (LANES - g * EXPERTS_PER_GROUP) % LANES
        w_g = cw if shift == 0 else pltpu.roll(cw, shift, axis=1)
        slot = jnp.where(lane_i < EXPERTS_PER_GROUP, w_g,
                         jnp.where(lane_i == SLOT_IND, ind[:, g:g + 1],
                                   jnp.where(lane_i == SLOT_POS, pos[:, g:g + 1], 0.0)))
        slot_ref[:, g * LANES:(g + 1) * LANES] = slot
    info = jnp.where(lane_i < n_groups, ind, pltpu.roll(pos, n_groups, axis=1))
    info_t_ref[...] = info.T[0:2 * n_groups, :]


def _router(x2, gain, w_router, b_router, n_experts, n_groups):
    n, d = x2.shape
    tb = MOE_BLOCK
    tri = (jnp.arange(tb)[:, None] > jnp.arange(tb)[None, :]).astype(BF16)
    const = lambda a: pl.BlockSpec(a.shape, lambda i: (0, 0))
    return pl.pallas_call(
        functools.partial(_router_kernel, n_experts=n_experts, n_groups=n_groups),
        grid=(n // tb,),
        in_specs=[pl.BlockSpec((tb, d), lambda i: (i, 0)), const(gain), const(w_router), const(b_router),
                  pl.BlockSpec(tri.shape, lambda i: (0, 0), pipeline_mode=pl.Buffered(1))],
        out_specs=[pl.BlockSpec((tb, d), lambda i: (i, 0)),
                   pl.BlockSpec((tb, n_groups * LANES), lambda i: (i, 0)),
                   pl.BlockSpec((2 * n_groups, tb), lambda i: (0, i))],
        out_shape=[jax.ShapeDtypeStruct((n, d), BF16),
                   jax.ShapeDtypeStruct((n, n_groups * LANES), F32),
                   jax.ShapeDtypeStruct((2 * n_groups, n), F32)],
        compiler_params=_cparams("parallel"),
        name="router",
    )(x2, gain, w_router, b_router, tri)


def _moe_kernel(xn_ref, slot_ref, info_t_ref, wg_ref, wu_ref, wd_ref, o_ref, xc_scr, yc_scr, wc_scr, *, n_groups):
    g = pl.program_id(1)
    tb = xn_ref.shape[0]
    rt = MOE_ROW_TILE

    @pl.when(g == 0)
    def _():
        o_ref[...] = jnp.zeros_like(o_ref)

    slot = slot_ref[...]
    ind_col = slot[:, SLOT_IND:SLOT_IND + 1] > 0.5
    pos_col = slot[:, SLOT_POS:SLOT_POS + 1]
    ind_row = info_t_ref[pl.ds(g, 1), :]
    pos_row = info_t_ref[pl.ds(n_groups + g, 1), :]
    n_rows = jnp.sum(ind_row).astype(jnp.int32)
    n_tiles = (n_rows + rt - 1) // rt
    slot_hi = slot.astype(BF16)
    slot_lo = (slot - slot_hi.astype(F32)).astype(BF16)
    row_id = lax.broadcasted_iota(jnp.int32, (rt, tb), 0).astype(F32)
    col_id = lax.broadcasted_iota(jnp.int32, (tb, rt), 1).astype(F32)

    def rows(t):
        return pl.ds(pl.multiple_of(t * rt, rt), rt)

    def gather(t, _):
        base = (t * rt).astype(F32)
        pick = jnp.where((pos_row == row_id + base) & (ind_row > 0.5), 1.0, 0.0).astype(BF16)
        xc_scr[rows(t), :] = _dot(pick, xn_ref[...]).astype(BF16)
        wc_scr[rows(t), :] = _dot(pick, slot_hi) + _dot(pick, slot_lo)
        yc_scr[rows(t), :] = jnp.zeros((rt, yc_scr.shape[1]), F32)
        return 0

    lax.fori_loop(0, n_tiles, gather, 0)

    for e in range(EXPERTS_PER_GROUP):
        def expert(t, _, e=e):
            xg = xc_scr[rows(t), :]
            h = jax.nn.silu(_dot(xg, wg_ref[e])) * _dot(xg, wu_ref[e]) * wc_scr[rows(t), :][:, e:e + 1]
            yc_scr[rows(t), :] += _dot(h.astype(BF16), wd_ref[e])
            return 0

        lax.fori_loop(0, n_tiles, expert, 0)

    def scatter(t, _):
        base = (t * rt).astype(F32)
        put = jnp.where((pos_col == col_id + base) & ind_col, 1.0, 0.0).astype(BF16)
        o_ref[...] += _dot(put, yc_scr[rows(t), :].astype(BF16)).astype(BF16)
        return 0

    lax.fori_loop(0, n_tiles, scatter, 0)


def _moe(xn, slots, info_t, w_gate, w_up, w_down, n_groups):
    n, d = xn.shape
    f = w_gate.shape[2]
    tb = MOE_BLOCK
    epg = EXPERTS_PER_GROUP
    return pl.pallas_call(
        functools.partial(_moe_kernel, n_groups=n_groups),
        grid=(n // tb, n_groups),
        in_specs=[
            pl.BlockSpec((tb, d), lambda i, g: (i, 0)),
            pl.BlockSpec((tb, LANES), lambda i, g: (i, g)),
            pl.BlockSpec((2 * n_groups, tb), lambda i, g: (0, i)),
            pl.BlockSpec((epg, d, f), lambda i, g: (g, 0, 0)),
            pl.BlockSpec((epg, d, f), lambda i, g: (g, 0, 0)),
            pl.BlockSpec((epg, f, d), lambda i, g: (g, 0, 0)),
        ],
        out_specs=pl.BlockSpec((tb, d), lambda i, g: (i, 0)),
        out_shape=jax.ShapeDtypeStruct((n, d), BF16),
        scratch_shapes=[pltpu.VMEM((tb, d), BF16), pltpu.VMEM((tb, d), F32), pltpu.VMEM((tb, LANES), F32)],
        compiler_params=_cparams("parallel", "arbitrary"),
        name="moe",
    )(xn, slots, info_t, w_gate, w_up, w_down)


def _residual_kernel(x_ref, y_ref, o_ref):
    o_ref[...] = x_ref[...] + y_ref[...].astype(F32)


def _residual(x2, y2, tm=1024):
    n, d = x2.shape
    spec = pl.BlockSpec((tm, d), lambda i: (i, 0))
    return pl.pallas_call(
        _residual_kernel, grid=(n // tm,), in_specs=[spec, spec], out_specs=spec,
        out_shape=jax.ShapeDtypeStruct((n, d), F32), compiler_params=_cparams("parallel"), name="residual",
    )(x2, y2)


def _block_diag(w):
    nb, bs, _ = w.shape
    eye = jnp.eye(nb, dtype=w.dtype)
    return (eye[:, None, :, None] * w[:, :, None, :]).reshape(nb * bs, nb * bs)


def _overlap_t(n_sel, ncp):
    n = jnp.arange(ncp)
    j = jnp.arange(n_sel)
    c0 = n * CMP_STRIDE
    s0 = j * SEL_LEN
    ov = jnp.clip(jnp.minimum(c0[None, :] + CMP_LEN, s0[:, None] + SEL_LEN)
                  - jnp.maximum(c0[None, :], s0[:, None]), 0).astype(F32) / CMP_LEN
    return ov.astype(BF16)


def kernel(x, mix_norm, w_in, rnn_conv_w, rnn_conv_b, rg_wa, rg_ba, rg_wx, rg_bx, rg_lambda, cmp_pe, cmp_w1, cmp_b1, cmp_w2, cmp_b2, q_norm, k_norm, sc_conv_w, w_rnn_out, w_nsa_out, w_sc_out, w_out, ffn_norm, router_group_w, router_group_b, router_expert_w, router_expert_b, exp_w_gate, exp_w_up, exp_w_down):
    b, s, d = x.shape
    depth = w_in.shape[0]
    rnn_w = rnn_conv_w.shape[2]
    sc_w = sc_conv_w.shape[2]
    nsa_w = w_nsa_out.shape[1]
    dh = q_norm.shape[1]
    n_heads = nsa_w // dh
    kvh = n_heads // NSA_GROUP
    kv_w = kvh * dh
    n_groups = router_group_w.shape[2]
    n_experts = router_expert_w.shape[2]
    n = b * s
    assert rnn_w == COL_BLOCK and sc_w == COL_BLOCK and nsa_w == COL_BLOCK and dh == HEAD_DIM
    assert 4 * kv_w == COL_BLOCK and d % COL_BLOCK == 0
    assert s % COL_BLOCK == 0 and s >= WINDOW + Q_BLOCK and s % SEL_LEN == 0
    assert n_experts + n_groups <= LANES and n_experts == n_groups * EXPERTS_PER_GROUP

    nrow = s // CMP_STRIDE
    n_sel = s // SEL_LEN
    n_gate = n_heads * N_NSA_BRANCHES
    head_cols = 3 * COL_BLOCK + 6 * kv_w + n_gate
    pad_cols = 5 * COL_BLOCK - head_cols
    sc_block = 5
    gate_block = 8 * COL_BLOCK // d
    overlap_t = _overlap_t(n_sel, nrow)
    chunk = CMP_STRIDE * dh

    y_moe = None
    for l in range(depth):
        w_packed = jnp.concatenate(
            [w_in[l, :, :head_cols], jnp.zeros((d, pad_cols), F32), w_in[l, :, head_cols:]], axis=1).astype(BF16)
        proj, x2 = _inproj(x.reshape(n, d), y_moe, mix_norm[l][None, :], w_packed)
        x = x2.reshape(b, s, d)
        proj3 = proj.reshape(b, s, -1)

        y_rnn = _rnn_branch(
            proj3, rnn_conv_w[l], rnn_conv_b[l][None, :],
            _block_diag(rg_wa[l]).astype(BF16), rg_ba[l].reshape(1, rnn_w),
            _block_diag(rg_wx[l]).astype(BF16), rg_bx[l].reshape(1, rnn_w), rg_lambda[l][None, :])

        qt5 = proj3[:, :, 2 * COL_BLOCK:3 * COL_BLOCK].reshape(b, s // Q_BLOCK, Q_BLOCK, kvh, NSA_GROUP, dh)
        qt5 = qt5.transpose(0, 3, 1, 5, 4, 2).reshape(b, kvh, s // Q_BLOCK, dh, NSA_GROUP * Q_BLOCK)
        kv = proj3[:, :, 3 * COL_BLOCK:3 * COL_BLOCK + 6 * kv_w].reshape(b, s, 6, kvh, dh)
        kv_rows = kv.transpose(2, 0, 3, 1, 4)
        kv_cols = kv.transpose(2, 0, 3, 4, 1)
        kc_flat = kv_rows[0].reshape(b, kvh, nrow, chunk)
        vc_flat = kv_rows[1].reshape(b, kvh, nrow, chunk)
        pe_flat = cmp_pe[l].reshape(2, 2, chunk)
        k_cmp, v_cmp, k_sel, k_win = _kvprep(
            kc_flat, vc_flat, kv_rows[2], kv_rows[4], pe_flat, cmp_w1[l].astype(BF16), cmp_b1[l][:, None, :],
            cmp_w2[l].astype(BF16), cmp_b2[l][:, None, :], k_norm[l])
        g0 = 3 * COL_BLOCK + 6 * kv_w
        gates_t = proj3[:, :, g0:g0 + n_gate].reshape(b, s, kvh, NSA_GROUP * N_NSA_BRANCHES).transpose(0, 2, 3, 1)
        ones_rows = jnp.concatenate([jnp.ones((b, kvh, 1, s), BF16), jnp.zeros((b, kvh, 2 * SUBLANES - 1, s), BF16)],
                                    axis=2)
        v_sel_t = jnp.concatenate([kv_cols[3], ones_rows], axis=2)
        v_win_t = jnp.concatenate([kv_cols[5], ones_rows], axis=2)
        y_nsa = _nsa(qt5, k_cmp, v_cmp.transpose(0, 1, 3, 2), k_sel, v_sel_t, k_win, v_win_t, gates_t,
                     q_norm[l][:, None], overlap_t)

        x = _merge(x, proj3, y_rnn, y_nsa, sc_conv_w[l], w_rnn_out[l].astype(BF16), w_nsa_out[l].astype(BF16),
                   w_sc_out[l].astype(BF16), w_out[l].astype(BF16), sc_block, gate_block)

        w_router = jnp.concatenate(
            [router_expert_w[l], router_group_w[l], jnp.zeros((d, LANES - n_experts - n_groups), F32)],
            axis=1).astype(BF16)
        b_router = jnp.concatenate(
            [router_expert_b[l], router_group_b[l], jnp.zeros((LANES - n_experts - n_groups,), F32)])[None, :]
        xn, slots, info_t = _router(x.reshape(n, d), ffn_norm[l][None, :], w_router, b_router, n_experts, n_groups)
        y_moe = _moe(xn, slots, info_t, exp_w_gate[l].astype(BF16), exp_w_up[l].astype(BF16),
                     exp_w_down[l].astype(BF16), n_groups)
    return _residual(x.reshape(n, d), y_moe).reshape(b, s, d)
```

```python
import functools

import jax
import jax.numpy as jnp
from jax import lax
from jax.experimental import pallas as pl
from jax.experimental.pallas import tpu as pltpu

F32 = jnp.float32
BF16 = jnp.bfloat16

RMS_EPS = 1e-6
NEG_INF = -1e30
FORCE_SCORE = 1e6
RG_C = 8.0
HEAD_DIM = 64
NSA_GROUP = 4
N_NSA_BRANCHES = 3
CMP_LEN = 32
CMP_STRIDE = 16
SEL_LEN = 64
SEL_TOPK = 16
WINDOW = 512
Q_BLOCK = 256
EXPERTS_PER_GROUP = 8
TOPK_IN_GROUP = 2

LANES = 128
SUBLANES = 8
COL_BLOCK = 512
VMEM_LIMIT = 56 * 1024 * 1024
LOG2E = 1.4426950408889634


def _cparams(*sem):
    return pltpu.CompilerParams(dimension_semantics=sem, vmem_limit_bytes=VMEM_LIMIT)


def _rms(x, gain):
    return x * lax.rsqrt(jnp.mean(x * x, axis=-1, keepdims=True) + RMS_EPS) * gain


def _gelu_tanh(x):
    return 0.5 * x * (1.0 + jnp.tanh(0.7978845608028654 * (x + 0.044715 * x * x * x)))


def _dot(a, b):
    return jnp.dot(a, b, preferred_element_type=F32)


def _inproj_kernel(*refs, has_y, kvh):
    if has_y:
        x_ref, y_ref, g_ref, w_ref, wn_ref, o_ref, qt_ref, kr_ref, vt_ref, gt_ref, xo_ref = refs
        x = x_ref[...] + y_ref[...].astype(F32)
        xo_ref[...] = x
    else:
        x_ref, g_ref, w_ref, wn_ref, o_ref, qt_ref, kr_ref, vt_ref, gt_ref = refs
        x = x_ref[...]
    xn = _rms(x, g_ref[...]).astype(BF16)
    for c in range(w_ref.shape[1] // COL_BLOCK):
        sl = slice(c * COL_BLOCK, (c + 1) * COL_BLOCK)
        o_ref[:, sl] = _dot(xn, w_ref[:, sl]).astype(BF16)

    tm = x.shape[0]
    dh = HEAD_DIM
    tq = Q_BLOCK
    nsa = _dot(xn, wn_ref[...])
    for pair in range(kvh * NSA_GROUP // 2):
        slab_t = nsa[:, pair * LANES:(pair + 1) * LANES].T
        for half in range(2):
            head = 2 * pair + half
            g, h = head // NSA_GROUP, head % NSA_GROUP
            for j in range(tm // tq):
                qt_ref[g, j, :, h * tq:(h + 1) * tq] = slab_t[half * dh:(half + 1) * dh,
                                                             j * tq:(j + 1) * tq].astype(BF16)
    kv0 = kvh * NSA_GROUP * dh
    ones_rows = jnp.where(lax.broadcasted_iota(jnp.int32, (2 * SUBLANES, tm), 0) == 0, 1.0, 0.0).astype(BF16)
    for slab, (row_slot, col_slot) in enumerate(((0, None), (1, None), (2, None), (None, 0), (3, None), (None, 1))):
        v = nsa[:, kv0 + slab * kvh * dh:kv0 + (slab + 1) * kvh * dh]
        if row_slot is not None:
            for g in range(kvh):
                kr_ref[row_slot, g] = v[:, g * dh:(g + 1) * dh].astype(BF16)
        else:
            v_t = v.T
            for g in range(kvh):
                vt_ref[col_slot, g, 0:dh, :] = v_t[g * dh:(g + 1) * dh].astype(BF16)
                vt_ref[col_slot, g, dh:dh + 2 * SUBLANES, :] = ones_rows
    g0 = kv0 + 6 * kvh * dh
    gates_t = nsa[:, g0:g0 + LANES].T
    per_group = NSA_GROUP * N_NSA_BRANCHES
    for g in range(kvh):
        gt_ref[g] = gates_t[g * per_group:(g + 1) * per_group].astype(BF16)


def _inproj(x3, y3, gain, w_main, w_nsa, kvh, tm=512):
    b, s, d = x3.shape
    wp = w_main.shape[1]
    dh = HEAD_DIM
    row = pl.BlockSpec((None, tm, d), lambda bi, ti: (bi, ti, 0))
    has_y = y3 is not None
    lanes = NSA_GROUP * Q_BLOCK
    per_group = NSA_GROUP * N_NSA_BRANCHES
    out_specs = [
        pl.BlockSpec((None, tm, wp), lambda bi, ti: (bi, ti, 0)),
        pl.BlockSpec((None, kvh, tm // Q_BLOCK, dh, lanes), lambda bi, ti: (bi, 0, ti, 0, 0)),
        pl.BlockSpec((4, None, kvh, tm, dh), lambda bi, ti: (0, bi, 0, ti, 0)),
        pl.BlockSpec((2, None, kvh, dh + 2 * SUBLANES, tm), lambda bi, ti: (0, bi, 0, 0, ti)),
        pl.BlockSpec((None, kvh, per_group, tm), lambda bi, ti: (bi, 0, 0, ti)),
    ] + ([row] if has_y else [])
    out_shape = [
        jax.ShapeDtypeStruct((b, s, wp), BF16),
        jax.ShapeDtypeStruct((b, kvh, s // Q_BLOCK, dh, lanes), BF16),
        jax.ShapeDtypeStruct((4, b, kvh, s, dh), BF16),
        jax.ShapeDtypeStruct((2, b, kvh, dh + 2 * SUBLANES, s), BF16),
        jax.ShapeDtypeStruct((b, kvh, per_group, s), BF16),
    ] + ([jax.ShapeDtypeStruct((b, s, d), F32)] if has_y else [])
    const = lambda a: pl.BlockSpec(a.shape, lambda bi, ti: (0, 0), pipeline_mode=pl.Buffered(1))
    res = pl.pallas_call(
        functools.partial(_inproj_kernel, has_y=has_y, kvh=kvh),
        grid=(b, s // tm),
        in_specs=[row] + ([row] if has_y else []) + [pl.BlockSpec((1, d), lambda bi, ti: (0, 0)),
                                                    const(w_main), const(w_nsa)],
        out_specs=out_specs,
        out_shape=out_shape,
        compiler_params=_cparams("parallel", "parallel"),
        name="inproj",
    )(*([x3, y3] if has_y else [x3]), gain, w_main, w_nsa)
    return tuple(res[:5]) + ((res[5],) if has_y else (x3,))


def _rnn_kernel(x_ref, xp_ref, y_ref, cw_ref, cb_ref, wa_ref, ba_ref, wx_ref, bx_ref, lam_ref,
                o_ref, h_scr):
    ti = pl.program_id(1)
    ts = x_ref.shape[0]
    kw = cw_ref.shape[0]

    @pl.when(ti == 0)
    def _():
        h_scr[...] = jnp.zeros_like(h_scr)

    x = x_ref[...].astype(F32)
    halo = jnp.where(ti == 0, 0.0, xp_ref[...].astype(F32))
    groups_x = jnp.concatenate([halo, x], axis=0).reshape(ts // SUBLANES + 1, SUBLANES, x.shape[1])
    row_in = lax.broadcasted_iota(jnp.int32, (ts // SUBLANES, SUBLANES, x.shape[1]), 1)
    u = cb_ref[...] + cw_ref[kw - 1:kw, :] * groups_x[1:]
    for shift in range(1, kw):
        rot = pltpu.roll(groups_x, shift, axis=1)
        u = u + cw_ref[kw - 1 - shift:kw - shift, :] * jnp.where(row_in >= shift, rot[1:], rot[:-1])
    u = u.reshape(ts, x.shape[1])
    ub = u.astype(BF16)
    r = jax.nn.sigmoid(_dot(ub, wa_ref[...]) + ba_ref[...])
    gi = jax.nn.sigmoid(_dot(ub, wx_ref[...]) + bx_ref[...])
    lam = lam_ref[...]
    softplus_neg = jnp.maximum(-lam, 0.0) + jnp.log(1.0 + jnp.exp(-jnp.abs(lam)))
    log_a = -RG_C * r * softplus_neg
    a = jnp.exp(log_a)
    b = jnp.sqrt(1.0 - jnp.exp(2.0 * log_a)) * (gi * u)

    a = a.reshape(ts // SUBLANES, SUBLANES, a.shape[1])
    b = b.reshape(a.shape)
    row_in_group = lax.broadcasted_iota(jnp.int32, a.shape, 1)
    d = 1
    while d < SUBLANES:
        keep = row_in_group >= d
        a_sh = pltpu.roll(a, d, axis=1)
        b_sh = pltpu.roll(b, d, axis=1)
        b = jnp.where(keep, a * b_sh + b, b)
        a = jnp.where(keep, a * a_sh, a)
        d *= 2
    h_prev = h_scr[...]
    groups = []
    for r in range(ts // SUBLANES):
        h_r = a[r] * h_prev + b[r]
        groups.append(h_r)
        h_prev = h_r[SUBLANES - 1:SUBLANES, :]
    h = jnp.concatenate(groups, axis=0)
    h_scr[...] = h_prev
    o_ref[...] = (_gelu_tanh(y_ref[...].astype(F32)) * h).astype(BF16)


def _rnn_branch(proj3, conv_w, conv_b, wa_bd, ba, wx_bd, bx, lam, ts=512):
    b, s, _ = proj3.shape
    w = conv_w.shape[1]
    nb = w // COL_BLOCK
    rows_per = ts // SUBLANES
    const = lambda shape: pl.BlockSpec(shape, lambda bi, ti: (0, 0))
    return pl.pallas_call(
        _rnn_kernel,
        grid=(b, s // ts),
        in_specs=[
            pl.BlockSpec((None, ts, w), lambda bi, ti: (bi, ti, 0)),
            pl.BlockSpec((None, SUBLANES, w), lambda bi, ti: (bi, jnp.maximum(ti * rows_per - 1, 0), 0)),
            pl.BlockSpec((None, ts, w), lambda bi, ti: (bi, ti, nb)),
            const(conv_w.shape), const((1, w)), const((w, w)), const((1, w)), const((w, w)), const((1, w)),
            const((1, w)),
        ],
        out_specs=pl.BlockSpec((None, ts, w), lambda bi, ti: (bi, ti, 0)),
        out_shape=jax.ShapeDtypeStruct((b, s, w), BF16),
        scratch_shapes=[pltpu.VMEM((1, w), F32)],
        compiler_params=_cparams("parallel", "arbitrary"),
        name="rnn_branch",
    )(proj3, proj3, proj3, conv_w, conv_b, wa_bd, ba, wx_bd, bx, lam)


def _kvprep_kernel(kcf_ref, vcf_ref, ks_ref, kw_ref, pe_ref, w1_ref, b1_ref, w2_ref, b2_ref, kn_ref,
                   kc_o, vc_o, ks_o, kw_o):
    half = kcf_ref.shape[1]
    nrow = kcf_ref.shape[0]

    def compress(flat_ref, i):
        xf = flat_ref[...].astype(F32)
        xa = (xf + pe_ref[i, 0:1, :]).astype(BF16)
        xb = (xf + pe_ref[i, 1:2, :]).astype(BF16)
        first = _dot(xa, w1_ref[i, 0:half, :])
        second = _dot(xb, w1_ref[i, half:2 * half, :])
        hid = first + pltpu.roll(second, nrow - 1, axis=0) + b1_ref[i]
        hid = _gelu_tanh(hid).astype(BF16)
        return _dot(hid, w2_ref[i]) + b2_ref[i]

    kc_o[...] = _rms(compress(kcf_ref, 0), kn_ref[0:1, :]).astype(BF16)
    vc_o[...] = compress(vcf_ref, 1).astype(BF16)
    ks = _rms(ks_ref[...].astype(F32), kn_ref[1:2, :])
    row = lax.broadcasted_iota(jnp.int32, ks.shape, 0)
    col = lax.broadcasted_iota(jnp.int32, ks.shape, 1)
    block_hot = jnp.where((row // SEL_LEN) % SUBLANES == col, 1.0, 0.0)
    ks_o[...] = jnp.concatenate([ks, block_hot], axis=1).astype(BF16)
    kw_o[...] = _rms(kw_ref[...].astype(F32), kn_ref[2:3, :]).astype(BF16)


def _kvprep(kc_flat, vc_flat, ks, kw, pe_flat, w1, b1, w2, b2, k_norm):
    b, g, nrow, half = kc_flat.shape
    s = ks.shape[2]
    dh = ks.shape[3]
    bg = lambda shape: pl.BlockSpec((None, None) + shape, lambda bi, gi: (bi, gi, 0, 0))
    full = lambda a: pl.BlockSpec(a.shape, lambda bi, gi: (0,) * a.ndim)
    return pl.pallas_call(
        _kvprep_kernel,
        grid=(b, g),
        in_specs=[bg((nrow, half)), bg((nrow, half)), bg((s, dh)), bg((s, dh)),
                  full(pe_flat), full(w1), full(b1), full(w2), full(b2), full(k_norm)],
        out_specs=[bg((nrow, dh)), bg((nrow, dh)), bg((s, 2 * dh)), bg((s, dh))],
        out_shape=[jax.ShapeDtypeStruct((b, g, nrow, dh), BF16), jax.ShapeDtypeStruct((b, g, nrow, dh), BF16),
                   jax.ShapeDtypeStruct((b, g, s, 2 * dh), BF16), jax.ShapeDtypeStruct((b, g, s, dh), BF16)],
        compiler_params=_cparams("parallel", "parallel"),
        name="kvprep",
    )(kc_flat, vc_flat, ks, kw, pe_flat, w1, b1, w2, b2, k_norm)


def _nsa_kernel(qt_ref, kc_ref, vct_ref, ks_ref, vst_ref, kw_ref, vwt_ref, gt_ref, qn_ref, ov_ref, cb_ref, bb_ref,
                o_ref, sel_scr, *, key_tile):
    c = pl.program_id(2)
    tq = Q_BLOCK
    t0 = c * tq
    n_sel = ov_ref.shape[0]
    ncp = kc_ref.shape[0]
    top_k = min(SEL_TOPK, n_sel)
    lanes = NSA_GROUP * tq
    blocks_per_tile = key_tile // SEL_LEN

    qf = qt_ref[...].astype(F32)
    qn = qf * lax.rsqrt(jnp.mean(qf * qf, axis=0, keepdims=True) + RMS_EPS) * qn_ref[...]
    qs = (qn * (HEAD_DIM ** -0.5 * LOG2E)).astype(BF16)
    t_lane = t0 + (lax.broadcasted_iota(jnp.int32, (1, lanes), 1) & (tq - 1))

    n_tiles = (t0 + tq + key_tile - 1) // key_tile
    n_full = n_tiles - 1

    pad_rows = ks_ref.shape[1] - HEAD_DIM - 2 * SUBLANES
    assert blocks_per_tile == SUBLANES and pad_rows >= 0

    def scores(kt, bias_rows=None):
        k0 = pl.multiple_of(kt * key_tile, key_tile)
        if bias_rows is None:
            bias_rows = jnp.zeros((SUBLANES, lanes), F32)
        extra = jnp.concatenate([bias_rows, jnp.zeros((SUBLANES, lanes), F32)], axis=0).astype(BF16)
        q_aug = jnp.concatenate([qs, extra, jnp.zeros((pad_rows, lanes), BF16)], axis=0)
        return _dot(ks_ref[pl.ds(k0, key_tile), :], q_aug)

    wl = WINDOW + tq
    w0 = pl.multiple_of(jnp.maximum(t0 - WINDOW, 0), tq)
    s_c = _dot(kc_ref[...], qs)
    s_w = _dot(kw_ref[pl.ds(w0, wl), :], qs)
    s_l = scores(n_full)

    off_c = pl.multiple_of(cb_ref.shape[0] - ncp - t0 // CMP_STRIDE, SUBLANES)
    s_c = s_c + cb_ref[pl.ds(off_c, ncp), :]
    e_c = jnp.exp2(s_c - jnp.max(s_c, axis=0, keepdims=True))
    inv_c = 1.0 / jnp.sum(e_c, axis=0, keepdims=True)
    p_c = e_c * jnp.where(t_lane >= CMP_LEN - 1, inv_c, 0.0)
    o_c = _dot(vct_ref[...], p_c.astype(BF16))

    j_i = lax.broadcasted_iota(jnp.int32, (n_sel, tq), 0)
    cur = (t0 + lax.broadcasted_iota(jnp.int32, (n_sel, tq), 1)) // SEL_LEN
    ps = p_c[:, 0:tq]
    for h in range(1, NSA_GROUP):
        ps = ps + p_c[:, h * tq:(h + 1) * tq]
    ps_hi = ps.astype(BF16)
    ps_lo = (ps - ps_hi.astype(F32)).astype(BF16)
    imp = _dot(ov_ref[...], ps_hi) + _dot(ov_ref[...], ps_lo)
    forced = (j_i == 0) | (j_i == cur) | (j_i == cur - 1)
    imp = jnp.where(forced, FORCE_SCORE, imp)
    imp = jnp.where(j_i <= cur, imp, -1.0)
    slabs = [imp[v * SUBLANES:(v + 1) * SUBLANES, :] for v in range(n_sel // SUBLANES)]
    j_slab = lax.broadcasted_iota(jnp.int32, (SUBLANES, tq), 0)
    cnts = [jnp.zeros((SUBLANES, tq), jnp.int32) for _ in slabs]
    for jp in range(n_sel):
        r = imp[jp:jp + 1, :]
        for v, slab in enumerate(slabs):
            lo = v * SUBLANES
            if lo > jp:
                before = r >= slab
            elif lo + SUBLANES - 1 <= jp:
                before = r > slab
            else:
                before = (r > slab) | ((r == slab) & (j_slab + lo > jp))
            cnts[v] = cnts[v] + before.astype(jnp.int32)
    cnt = jnp.concatenate(cnts, axis=0)
    sel_bias = jnp.where((cnt < top_k) & (imp >= 0.0), 0.0, NEG_INF)
    sel_scr[...] = jnp.concatenate([sel_bias] * NSA_GROUP, axis=1)

    off_w = pl.multiple_of(WINDOW - (t0 - w0), tq)
    s_w = s_w + bb_ref[pl.ds(off_w, wl), :]
    e_w = jnp.exp2(s_w - jnp.max(s_w, axis=0, keepdims=True))
    a_w = _dot(vwt_ref[:, pl.ds(w0, wl)], e_w.astype(BF16))
    o_w = a_w[0:HEAD_DIM] * (1.0 / a_w[HEAD_DIM:HEAD_DIM + 1])

    def bias_rows(kt):
        return sel_scr[pl.ds(pl.multiple_of(kt * blocks_per_tile, SUBLANES), blocks_per_tile), :]

    def update(carry, kt, s):
        m_o, a_o = carry
        k0 = pl.multiple_of(kt * key_tile, key_tile)
        m_n = jnp.maximum(m_o, jnp.max(s, axis=0, keepdims=True))
        p = jnp.exp2(s - m_n).astype(BF16)
        a_n = jnp.exp2(m_o - m_n) * a_o + _dot(vst_ref[:, pl.ds(k0, key_tile)], p)
        return m_n, a_n

    def init():
        return jnp.full((1, lanes), NEG_INF, F32), jnp.zeros((vst_ref.shape[0], lanes), F32)

    off_l = pl.multiple_of(WINDOW - (t0 - n_full * key_tile), tq)
    last_rows = bias_rows(n_full)
    last_bias = jnp.concatenate(
        [jnp.broadcast_to(last_rows[jb:jb + 1, :], (SEL_LEN, lanes)) for jb in range(blocks_per_tile)], axis=0)
    c_first = update(init(), n_full, s_l + last_bias + bb_ref[pl.ds(off_l, key_tile), :])

    def pair_step(i, carry):
        c_a, c_b = carry
        s_a = scores(2 * i, bias_rows(2 * i))
        s_b = scores(2 * i + 1, jnp.where(2 * i + 1 < n_full, bias_rows(2 * i + 1), NEG_INF))
        return update(c_a, 2 * i, s_a), update(c_b, 2 * i + 1, s_b)

    (m_a, a_a), (m_b, a_b) = lax.fori_loop(0, (n_full + 1) // 2, pair_step, (c_first, init()))
    m_ab = jnp.maximum(m_a, m_b)
    a_s = jnp.exp2(m_a - m_ab) * a_a + jnp.exp2(m_b - m_ab) * a_b
    o_s = a_s[0:HEAD_DIM] * (1.0 / a_s[HEAD_DIM:HEAD_DIM + 1])

    gates = jax.nn.sigmoid(gt_ref[...].astype(F32))
    for h in range(NSA_GROUP):
        hs = slice(h * tq, (h + 1) * tq)
        gb = h * N_NSA_BRANCHES
        o = (gates[gb:gb + 1, :] * o_c[:, hs] + gates[gb + 1:gb + 2, :] * o_s[:, hs]
             + gates[gb + 2:gb + 3, :] * o_w[:, hs])
        o_ref[:, h * HEAD_DIM:(h + 1) * HEAD_DIM] = o.T.astype(BF16)


def _band_bias(n_rows, lanes):
    i = jnp.arange(n_rows)[:, None]
    tl = (jnp.arange(lanes) % Q_BLOCK)[None, :]
    return jnp.where((tl < i) & (i <= tl + WINDOW), 0.0, NEG_INF).astype(F32)


def _cmp_bias(n_lead, ncp, lanes):
    i = jnp.arange(n_lead + ncp)[:, None]
    tl = (jnp.arange(lanes) % Q_BLOCK)[None, :]
    return jnp.where(CMP_STRIDE * (i - n_lead) + CMP_LEN - 1 <= tl, 0.0, NEG_INF).astype(F32)


def _nsa(qt5, k_cmp, v_cmp_t, k_sel, v_sel_t, k_win, v_win_t, gates_t, q_norm_col, overlap_t, key_tile=512):
    b, g, nchunk, dh, lanes = qt5.shape
    s = k_sel.shape[2]
    ncp = k_cmp.shape[2]
    ng = gates_t.shape[2]
    n_sel = overlap_t.shape[0]
    assert key_tile <= WINDOW and key_tile % Q_BLOCK == 0 and s % key_tile == 0
    cmp_bias = _cmp_bias((s - Q_BLOCK) // CMP_STRIDE, ncp, lanes)
    band_bias = _band_bias(2 * WINDOW + Q_BLOCK, lanes)
    bg = lambda shape: pl.BlockSpec((None, None) + shape, lambda bi, gi, ci: (bi, gi, 0, 0))
    const = lambda a: pl.BlockSpec(a.shape, lambda bi, gi, ci: (0, 0), pipeline_mode=pl.Buffered(1))
    return pl.pallas_call(
        functools.partial(_nsa_kernel, key_tile=key_tile),
        grid=(b, g, nchunk),
        in_specs=[
            pl.BlockSpec((None, None, None, dh, lanes), lambda bi, gi, ci: (bi, gi, ci, 0, 0)),
            bg((ncp, dh)), bg((dh, ncp)), bg((s, k_sel.shape[3])), bg((v_sel_t.shape[2], s)), bg((s, dh)),
            bg((v_win_t.shape[2], s)),
            pl.BlockSpec((None, None, ng, Q_BLOCK), lambda bi, gi, ci: (bi, gi, 0, ci)),
            pl.BlockSpec(q_norm_col.shape, lambda bi, gi, ci: (0, 0)),
            pl.BlockSpec(overlap_t.shape, lambda bi, gi, ci: (0, 0)),
            const(cmp_bias), const(band_bias),
        ],
        out_specs=pl.BlockSpec((None, Q_BLOCK, NSA_GROUP * dh), lambda bi, gi, ci: (bi, ci, gi)),
        out_shape=jax.ShapeDtypeStruct((b, s, g * NSA_GROUP * dh), BF16),
        scratch_shapes=[pltpu.VMEM((n_sel, lanes), F32)],
        compiler_params=_cparams("parallel", "parallel", "arbitrary"),
        name="nsa",
    )(qt5, k_cmp, v_cmp_t, k_sel, v_sel_t, k_win, v_win_t, gates_t, q_norm_col, overlap_t, cmp_bias, band_bias)


def _merge_kernel(x_ref, yr_ref, yn_ref, sb_ref, sc_ref, sh_ref, scp_ref, shp_ref, g0_ref, g1_ref, g2_ref,
                  cw_ref, wr_ref, wn_ref, ws_ref, wo_ref, o_ref):
    ti = pl.program_id(1)
    tm = x_ref.shape[0]
    kw = cw_ref.shape[0]
    prod = sc_ref[...].astype(F32) * sh_ref[...].astype(F32)
    halo = jnp.where(ti == 0, 0.0, scp_ref[...].astype(F32) * shp_ref[...].astype(F32))
    groups_p = jnp.concatenate([halo, prod], axis=0).reshape(tm // SUBLANES + 1, SUBLANES, prod.shape[1])
    row_in = lax.broadcasted_iota(jnp.int32, (tm // SUBLANES, SUBLANES, prod.shape[1]), 1)
    conv = cw_ref[kw - 1:kw, :] * groups_p[1:]
    for shift in range(1, kw):
        rot = pltpu.roll(groups_p, shift, axis=1)
        conv = conv + cw_ref[kw - 1 - shift:kw - shift, :] * jnp.where(row_in >= shift, rot[1:], rot[:-1])
    y_sc = (sb_ref[...].astype(F32) * conv.reshape(tm, prod.shape[1])).astype(BF16)
    merged = (jax.nn.sigmoid(g0_ref[...].astype(F32)) * _dot(yr_ref[...], wr_ref[...])
              + jax.nn.sigmoid(g1_ref[...].astype(F32)) * _dot(yn_ref[...], wn_ref[...])
              + jax.nn.sigmoid(g2_ref[...].astype(F32)) * _dot(y_sc, ws_ref[...]))
    o_ref[...] = x_ref[...] + _dot(merged.astype(BF16), wo_ref[...])


def _merge(x3, proj3, y_rnn, y_nsa, sc_conv_w, w_rnn_out, w_nsa_out, w_sc_out, w_out, sc_block, gate_block, tm=512):
    b, s, d = x3.shape
    w = y_rnn.shape[2]
    rows_per = tm // SUBLANES
    tile = lambda width, col: pl.BlockSpec((None, tm, width), lambda bi, ti: (bi, ti, col))
    prev = lambda col: pl.BlockSpec((None, SUBLANES, w),
                                    lambda bi, ti: (bi, jnp.maximum(ti * rows_per - 1, 0), col))
    const = lambda a: pl.BlockSpec(a.shape, lambda bi, ti: (0, 0))
    return pl.pallas_call(
        _merge_kernel,
        grid=(b, s // tm),
        in_specs=[
            tile(d, 0), tile(w, 0), tile(w, 0),
            tile(w, sc_block), tile(w, sc_block + 1), tile(w, sc_block + 2),
            prev(sc_block + 1), prev(sc_block + 2),
            tile(d, gate_block), tile(d, gate_block + 1), tile(d, gate_block + 2),
            const(sc_conv_w), const(w_rnn_out), const(w_nsa_out), const(w_sc_out), const(w_out),
        ],
        out_specs=tile(d, 0),
        out_shape=jax.ShapeDtypeStruct((b, s, d), F32),
        compiler_params=_cparams("parallel", "parallel"),
        name="merge",
    )(x3, y_rnn, y_nsa, proj3, proj3, proj3, proj3, proj3, proj3, proj3, proj3,
      sc_conv_w, w_rnn_out, w_nsa_out, w_sc_out, w_out)


MOE_BLOCK = 1024
MOE_ROW_TILE = 304
SLOT_IND = EXPERTS_PER_GROUP
SLOT_POS = EXPERTS_PER_GROUP + 1


def _route(logits, n_experts, n_groups):
    lane_i = lax.broadcasted_iota(jnp.int32, logits.shape, 1)
    lane = lane_i.astype(F32)
    big = float(LANES)
    is_g = (lane_i >= n_experts) & (lane_i < n_experts + n_groups)
    gl = jnp.where(is_g, logits, NEG_INF)
    gmax = jnp.max(gl, axis=-1, keepdims=True)
    grp = jnp.min(jnp.where(is_g & (gl == gmax), lane - n_experts, big), axis=-1, keepdims=True)
    p_grp = 1.0 / jnp.sum(jnp.where(is_g, jnp.exp(gl - gmax), 0.0), axis=-1, keepdims=True)
    in_g = (lane_i < n_experts) & ((lane_i // EXPERTS_PER_GROUP).astype(F32) == grp)
    el = jnp.where(in_g, logits, NEG_INF)
    ee = jnp.where(in_g, jnp.exp(el - jnp.max(el, axis=-1, keepdims=True)), 0.0)
    ep = ee / jnp.sum(ee, axis=-1, keepdims=True)
    top1 = jnp.max(jnp.where(in_g, ep, -1.0), axis=-1, keepdims=True)
    i1 = jnp.min(jnp.where(in_g & (ep == top1), lane, big), axis=-1, keepdims=True)
    rest = jnp.where(in_g & (lane != i1), ep, -1.0)
    top2 = jnp.max(rest, axis=-1, keepdims=True)
    i2 = jnp.min(jnp.where(rest == top2, lane, big), axis=-1, keepdims=True)
    denom = top1 + top2
    cw = (jnp.where(lane == i1, p_grp * top1 / denom, 0.0)
          + jnp.where(lane == i2, p_grp * top2 / denom, 0.0))
    return cw, grp


def _router_kernel(x_ref, g_ref, wr_ref, br_ref, tri_ref, xn_ref, slot_ref, info_t_ref, *, n_experts, n_groups):
    xn = _rms(x_ref[...], g_ref[...]).astype(BF16)
    xn_ref[...] = xn
    cw, grp = _route(_dot(xn, wr_ref[...]) + br_ref[...], n_experts, n_groups)
    lane_i = lax.broadcasted_iota(jnp.int32, cw.shape, 1)
    lane = lane_i.astype(F32)
    ind = jnp.where((lane == grp) & (lane_i < n_groups), 1.0, 0.0)
    pos = _dot(tri_ref[...], ind.astype(BF16))
    for g in range(n_groups):
        shift = (LANES - g * EXPERTS_PER_GROUP) % LANES
        w_g = cw if shift == 0 else pltpu.roll(cw, shift, axis=1)
        slot = jnp.where(lane_i < EXPERTS_PER_GROUP, w_g,
                         jnp.where(lane_i == SLOT_IND, ind[:, g:g + 1],
                                   jnp.where(lane_i == SLOT_POS, pos[:, g:g + 1], 0.0)))
        slot_ref[:, g * LANES:(g + 1) * LANES] = slot
    info = jnp.where(lane_i < n_groups, ind, pltpu.roll(pos, n_groups, axis=1))
    info_t_ref[...] = info.T[0:2 * n_groups, :]


def _router(x2, gain, w_router, b_router, n_experts, n_groups):
    n, d = x2.shape
    tb = MOE_BLOCK
    tri = (jnp.arange(tb)[:, None] > jnp.arange(tb)[None, :]).astype(BF16)
    const = lambda a: pl.BlockSpec(a.shape, lambda i: (0, 0))
    return pl.pallas_call(
        functools.partial(_router_kernel, n_experts=n_experts, n_groups=n_groups),
        grid=(n // tb,),
        in_specs=[pl.BlockSpec((tb, d), lambda i: (i, 0)), const(gain), const(w_router), const(b_router),
                  pl.BlockSpec(tri.shape, lambda i: (0, 0), pipeline_mode=pl.Buffered(1))],
        out_specs=[pl.BlockSpec((tb, d), lambda i: (i, 0)),
                   pl.BlockSpec((tb, n_groups * LANES), lambda i: (i, 0)),
                   pl.BlockSpec((2 * n_groups, tb), lambda i: (0, i))],
        out_shape=[jax.ShapeDtypeStruct((n, d), BF16),
                   jax.ShapeDtypeStruct((n, n_groups * LANES), F32),
                   jax.ShapeDtypeStruct((2 * n_groups, n), F32)],
        compiler_params=_cparams("parallel"),
        name="router",
    )(x2, gain, w_router, b_router, tri)


def _moe_kernel(xn_ref, slot_ref, info_t_ref, wg_ref, wu_ref, wd_ref, o_ref, xc_scr, yc_scr, wc_scr, *, n_groups):
    g = pl.program_id(1)
    tb = xn_ref.shape[0]
    rt = MOE_ROW_TILE

    @pl.when(g == 0)
    def _():
        o_ref[...] = jnp.zeros_like(o_ref)

    slot = slot_ref[...]
    ind_col = slot[:, SLOT_IND:SLOT_IND + 1] > 0.5
    pos_col = slot[:, SLOT_POS:SLOT_POS + 1]
    ind_row = info_t_ref[pl.ds(g, 1), :]
    pos_row = info_t_ref[pl.ds(n_groups + g, 1), :]
    n_rows = jnp.sum(ind_row).astype(jnp.int32)
    n_tiles = (n_rows + rt - 1) // rt
    slot_hi = slot.astype(BF16)
    slot_lo = (slot - slot_hi.astype(F32)).astype(BF16)
    row_id = lax.broadcasted_iota(jnp.int32, (rt, tb), 0).astype(F32)
    col_id = lax.broadcasted_iota(jnp.int32, (tb, rt), 1).astype(F32)

    def rows(t):
        return pl.ds(pl.multiple_of(t * rt, 2 * SUBLANES), rt)

    def gather(t, _):
        base = (t * rt).astype(F32)
        pick = jnp.where((pos_row == row_id + base) & (ind_row > 0.5), 1.0, 0.0).astype(BF16)
        xc_scr[rows(t), :] = _dot(pick, xn_ref[...]).astype(BF16)
        wc_scr[rows(t), :] = _dot(pick, slot_hi) + _dot(pick, slot_lo)
        yc_scr[rows(t), :] = jnp.zeros((rt, yc_scr.shape[1]), F32)
        return 0

    lax.fori_loop(0, n_tiles, gather, 0)

    for e in range(EXPERTS_PER_GROUP):
        def expert(t, _, e=e):
            xg = xc_scr[rows(t), :]
            h = jax.nn.silu(_dot(xg, wg_ref[e])) * _dot(xg, wu_ref[e]) * wc_scr[rows(t), :][:, e:e + 1]
            yc_scr[rows(t), :] += _dot(h.astype(BF16), wd_ref[e])
            return 0

        lax.fori_loop(0, n_tiles, expert, 0)

    def scatter(t, _):
        base = (t * rt).astype(F32)
        put = jnp.where((pos_col == col_id + base) & ind_col, 1.0, 0.0).astype(BF16)
        o_ref[...] += _dot(put, yc_scr[rows(t), :].astype(BF16)).astype(BF16)
        return 0

    lax.fori_loop(0, n_tiles, scatter, 0)


def _moe(xn, slots, info_t, w_gate, w_up, w_down, n_groups):
    n, d = xn.shape
    f = w_gate.shape[2]
    tb = MOE_BLOCK
    epg = EXPERTS_PER_GROUP
    assert MOE_ROW_TILE % (2 * SUBLANES) == 0
    cap = -(-tb // MOE_ROW_TILE) * MOE_ROW_TILE
    return pl.pallas_call(
        functools.partial(_moe_kernel, n_groups=n_groups),
        grid=(n // tb, n_groups),
        in_specs=[
            pl.BlockSpec((tb, d), lambda i, g: (i, 0)),
            pl.BlockSpec((tb, LANES), lambda i, g: (i, g)),
            pl.BlockSpec((2 * n_groups, tb), lambda i, g: (0, i)),
            pl.BlockSpec((epg, d, f), lambda i, g: (g, 0, 0)),
            pl.BlockSpec((epg, d, f), lambda i, g: (g, 0, 0)),
            pl.BlockSpec((epg, f, d), lambda i, g: (g, 0, 0)),
        ],
        out_specs=pl.BlockSpec((tb, d), lambda i, g: (i, 0)),
        out_shape=jax.ShapeDtypeStruct((n, d), BF16),
        scratch_shapes=[pltpu.VMEM((cap, d), BF16), pltpu.VMEM((cap, d), F32), pltpu.VMEM((cap, LANES), F32)],
        compiler_params=_cparams("parallel", "arbitrary"),
        name="moe",
    )(xn, slots, info_t, w_gate, w_up, w_down)


def _residual_kernel(x_ref, y_ref, o_ref):
    o_ref[...] = x_ref[...] + y_ref[...].astype(F32)


def _residual(x2, y2, tm=1024):
    n, d = x2.shape
    spec = pl.BlockSpec((tm, d), lambda i: (i, 0))
    return pl.pallas_call(
        _residual_kernel, grid=(n // tm,), in_specs=[spec, spec], out_specs=spec,
        out_shape=jax.ShapeDtypeStruct((n, d), F32), compiler_params=_cparams("parallel"), name="residual",
    )(x2, y2)


def _block_diag(w):
    nb, bs, _ = w.shape
    eye = jnp.eye(nb, dtype=w.dtype)
    return (eye[:, None, :, None] * w[:, :, None, :]).reshape(nb * bs, nb * bs)


def _overlap_t(n_sel, ncp):
    n = jnp.arange(ncp)
    j = jnp.arange(n_sel)
    c0 = n * CMP_STRIDE
    s0 = j * SEL_LEN
    ov = jnp.clip(jnp.minimum(c0[None, :] + CMP_LEN, s0[:, None] + SEL_LEN)
                  - jnp.maximum(c0[None, :], s0[:, None]), 0).astype(F32) / CMP_LEN
    return ov.astype(BF16)


def kernel(x, mix_norm, w_in, rnn_conv_w, rnn_conv_b, rg_wa, rg_ba, rg_wx, rg_bx, rg_lambda, cmp_pe, cmp_w1, cmp_b1, cmp_w2, cmp_b2, q_norm, k_norm, sc_conv_w, w_rnn_out, w_nsa_out, w_sc_out, w_out, ffn_norm, router_group_w, router_group_b, router_expert_w, router_expert_b, exp_w_gate, exp_w_up, exp_w_down):
    b, s, d = x.shape
    depth = w_in.shape[0]
    rnn_w = rnn_conv_w.shape[2]
    sc_w = sc_conv_w.shape[2]
    nsa_w = w_nsa_out.shape[1]
    dh = q_norm.shape[1]
    n_heads = nsa_w // dh
    kvh = n_heads // NSA_GROUP
    kv_w = kvh * dh
    n_groups = router_group_w.shape[2]
    n_experts = router_expert_w.shape[2]
    n = b * s
    assert rnn_w == COL_BLOCK and sc_w == COL_BLOCK and nsa_w == COL_BLOCK and dh == HEAD_DIM
    assert 4 * kv_w == COL_BLOCK and d % COL_BLOCK == 0
    assert s % COL_BLOCK == 0 and s >= WINDOW + Q_BLOCK and s % SEL_LEN == 0
    assert n_experts + n_groups <= LANES and n_experts == n_groups * EXPERTS_PER_GROUP

    nrow = s // CMP_STRIDE
    n_sel = s // SEL_LEN
    n_gate = n_heads * N_NSA_BRANCHES
    nsa_cols = nsa_w + 6 * kv_w + n_gate
    c_q = 2 * COL_BLOCK
    c_sc = c_q + nsa_cols
    c_mg = c_sc + 3 * COL_BLOCK
    gate_block = 2 * COL_BLOCK // d
    sc_block = (2 * COL_BLOCK + w_in.shape[2] - c_mg) // COL_BLOCK
    overlap_t = _overlap_t(n_sel, nrow)
    chunk = CMP_STRIDE * dh

    y_moe = None
    for l in range(depth):
        w_main = jnp.concatenate([w_in[l, :, :c_q], w_in[l, :, c_mg:], w_in[l, :, c_sc:c_mg]], axis=1).astype(BF16)
        w_nsa = jnp.concatenate(
            [w_in[l, :, c_q:c_sc], jnp.zeros((d, -nsa_cols % LANES), F32)], axis=1).astype(BF16)
        proj3, qt5, kv_rows, v_t, gates_t, x = _inproj(
            x, None if y_moe is None else y_moe.reshape(b, s, d), mix_norm[l][None, :], w_main, w_nsa, kvh)

        y_rnn = _rnn_branch(
            proj3, rnn_conv_w[l], rnn_conv_b[l][None, :],
            _block_diag(rg_wa[l]).astype(BF16), rg_ba[l].reshape(1, rnn_w),
            _block_diag(rg_wx[l]).astype(BF16), rg_bx[l].reshape(1, rnn_w), rg_lambda[l][None, :])

        kc_flat = kv_rows[0].reshape(b, kvh, nrow, chunk)
        vc_flat = kv_rows[1].reshape(b, kvh, nrow, chunk)
        pe_flat = cmp_pe[l].reshape(2, 2, chunk)
        k_cmp, v_cmp, k_sel, k_win = _kvprep(
            kc_flat, vc_flat, kv_rows[2], kv_rows[3], pe_flat, cmp_w1[l].astype(BF16), cmp_b1[l][:, None, :],
            cmp_w2[l].astype(BF16), cmp_b2[l][:, None, :], k_norm[l])
        y_nsa = _nsa(qt5, k_cmp, v_cmp.transpose(0, 1, 3, 2), k_sel, v_t[0], k_win, v_t[1], gates_t,
                     q_norm[l][:, None], overlap_t)

        x = _merge(x, proj3, y_rnn, y_nsa, sc_conv_w[l], w_rnn_out[l].astype(BF16), w_nsa_out[l].astype(BF16),
                   w_sc_out[l].astype(BF16), w_out[l].astype(BF16), sc_block, gate_block)

        w_router = jnp.concatenate(
            [router_expert_w[l], router_group_w[l], jnp.zeros((d, LANES - n_experts - n_groups), F32)],
            axis=1).astype(BF16)
        b_router = jnp.concatenate(
            [router_expert_b[l], router_group_b[l], jnp.zeros((LANES - n_experts - n_groups,), F32)])[None, :]
        xn, slots, info_t = _router(x.reshape(n, d), ffn_norm[l][None, :], w_router, b_router, n_experts, n_groups)
        y_moe = _moe(xn, slots, info_t, exp_w_gate[l].astype(BF16), exp_w_up[l].astype(BF16),
                     exp_w_down[l].astype(BF16), n_groups)
    return _residual(x.reshape(n, d), y_moe).reshape(b, s, d)
```

```python
import functools

import jax
import jax.numpy as jnp
from jax import lax
from jax.experimental import pallas as pl
from jax.experimental.pallas import tpu as pltpu

F32 = jnp.float32
BF16 = jnp.bfloat16

RMS_EPS = 1e-6
NEG_INF = -1e30
FORCE_SCORE = 1e6
RG_C = 8.0
HEAD_DIM = 64
NSA_GROUP = 4
N_NSA_BRANCHES = 3
CMP_LEN = 32
CMP_STRIDE = 16
SEL_LEN = 64
SEL_TOPK = 16
WINDOW = 512
Q_BLOCK = 256
EXPERTS_PER_GROUP = 8
TOPK_IN_GROUP = 2

LANES = 128
SUBLANES = 8
COL_BLOCK = 512
VMEM_LIMIT = 56 * 1024 * 1024
LOG2E = 1.4426950408889634


def _cparams(*sem):
    return pltpu.CompilerParams(dimension_semantics=sem, vmem_limit_bytes=VMEM_LIMIT)


def _rms(x, gain):
    return x * lax.rsqrt(jnp.mean(x * x, axis=-1, keepdims=True) + RMS_EPS) * gain


def _gelu_tanh(x):
    return 0.5 * x * (1.0 + jnp.tanh(0.7978845608028654 * (x + 0.044715 * x * x * x)))


def _dot(a, b):
    return jnp.dot(a, b, preferred_element_type=F32)


def _pack_kernel(w_ref, main_ref, nsa_ref, *, c_q, c_sc, c_mg):
    w = w_ref[...]
    n_mg = w.shape[1] - c_mg
    main_ref[:, 0:c_q] = w[:, 0:c_q].astype(BF16)
    main_ref[:, c_q:c_q + n_mg] = w[:, c_mg:].astype(BF16)
    main_ref[:, c_q + n_mg:] = w[:, c_sc:c_mg].astype(BF16)
    nsa_ref[:, 0:c_sc - c_q] = w[:, c_q:c_sc].astype(BF16)
    nsa_ref[:, c_sc - c_q:] = jnp.zeros((w.shape[0], nsa_ref.shape[1] - (c_sc - c_q)), BF16)


def _pack_w_in(w_in, c_q, c_sc, c_mg, tr=128):
    depth, d, win = w_in.shape
    w_main = win - (c_sc - c_q)
    w_nsa = -(-(c_sc - c_q) // LANES) * LANES
    return pl.pallas_call(
        functools.partial(_pack_kernel, c_q=c_q, c_sc=c_sc, c_mg=c_mg),
        grid=(depth, d // tr),
        in_specs=[pl.BlockSpec((None, tr, win), lambda l, i: (l, i, 0))],
        out_specs=[pl.BlockSpec((None, tr, w_main), lambda l, i: (l, i, 0)),
                   pl.BlockSpec((None, tr, w_nsa), lambda l, i: (l, i, 0))],
        out_shape=[jax.ShapeDtypeStruct((depth, d, w_main), BF16), jax.ShapeDtypeStruct((depth, d, w_nsa), BF16)],
        compiler_params=_cparams("parallel", "parallel"),
        name="pack_w_in",
    )(w_in)


def _inproj_kernel(*refs, has_y, kvh):
    if has_y:
        x_ref, y_ref, g_ref, w_ref, wn_ref, o_ref, qt_ref, kr_ref, vt_ref, gt_ref, xo_ref = refs
        x = x_ref[...] + y_ref[...].astype(F32)
        xo_ref[...] = x
    else:
        x_ref, g_ref, w_ref, wn_ref, o_ref, qt_ref, kr_ref, vt_ref, gt_ref = refs
        x = x_ref[...]
    xn = _rms(x, g_ref[...]).astype(BF16)
    for c in range(w_ref.shape[1] // COL_BLOCK):
        sl = slice(c * COL_BLOCK, (c + 1) * COL_BLOCK)
        o_ref[:, sl] = _dot(xn, w_ref[:, sl]).astype(BF16)

    tm = x.shape[0]
    dh = HEAD_DIM
    tq = Q_BLOCK
    nsa = _dot(xn, wn_ref[...])
    for pair in range(kvh * NSA_GROUP // 2):
        slab_t = nsa[:, pair * LANES:(pair + 1) * LANES].T
        for half in range(2):
            head = 2 * pair + half
            g, h = head // NSA_GROUP, head % NSA_GROUP
            for j in range(tm // tq):
                qt_ref[g, j, :, h * tq:(h + 1) * tq] = slab_t[half * dh:(half + 1) * dh,
                                                             j * tq:(j + 1) * tq].astype(BF16)
    kv0 = kvh * NSA_GROUP * dh
    ones_rows = jnp.where(lax.broadcasted_iota(jnp.int32, (2 * SUBLANES, tm), 0) == 0, 1.0, 0.0).astype(BF16)
    for slab, (row_slot, col_slot) in enumerate(((0, None), (1, None), (2, None), (None, 0), (3, None), (None, 1))):
        v = nsa[:, kv0 + slab * kvh * dh:kv0 + (slab + 1) * kvh * dh]
        if row_slot is not None:
            for g in range(kvh):
                kr_ref[row_slot, g] = v[:, g * dh:(g + 1) * dh].astype(BF16)
        else:
            v_t = v.T
            for g in range(kvh):
                vt_ref[col_slot, g, 0:dh, :] = v_t[g * dh:(g + 1) * dh].astype(BF16)
                vt_ref[col_slot, g, dh:dh + 2 * SUBLANES, :] = ones_rows
    g0 = kv0 + 6 * kvh * dh
    gates_t = nsa[:, g0:g0 + LANES].T
    per_group = NSA_GROUP * N_NSA_BRANCHES
    for g in range(kvh):
        gt_ref[g] = gates_t[g * per_group:(g + 1) * per_group].astype(BF16)


def _inproj(x3, y3, gain, w_main, w_nsa, kvh, tm=512):
    b, s, d = x3.shape
    wp = w_main.shape[1]
    dh = HEAD_DIM
    row = pl.BlockSpec((None, tm, d), lambda bi, ti: (bi, ti, 0))
    has_y = y3 is not None
    lanes = NSA_GROUP * Q_BLOCK
    per_group = NSA_GROUP * N_NSA_BRANCHES
    out_specs = [
        pl.BlockSpec((None, tm, wp), lambda bi, ti: (bi, ti, 0)),
        pl.BlockSpec((None, kvh, tm // Q_BLOCK, dh, lanes), lambda bi, ti: (bi, 0, ti, 0, 0)),
        pl.BlockSpec((4, None, kvh, tm, dh), lambda bi, ti: (0, bi, 0, ti, 0)),
        pl.BlockSpec((2, None, kvh, dh + 2 * SUBLANES, tm), lambda bi, ti: (0, bi, 0, 0, ti)),
        pl.BlockSpec((None, kvh, per_group, tm), lambda bi, ti: (bi, 0, 0, ti)),
    ] + ([row] if has_y else [])
    out_shape = [
        jax.ShapeDtypeStruct((b, s, wp), BF16),
        jax.ShapeDtypeStruct((b, kvh, s // Q_BLOCK, dh, lanes), BF16),
        jax.ShapeDtypeStruct((4, b, kvh, s, dh), BF16),
        jax.ShapeDtypeStruct((2, b, kvh, dh + 2 * SUBLANES, s), BF16),
        jax.ShapeDtypeStruct((b, kvh, per_group, s), BF16),
    ] + ([jax.ShapeDtypeStruct((b, s, d), F32)] if has_y else [])
    const = lambda a: pl.BlockSpec(a.shape, lambda bi, ti: (0, 0), pipeline_mode=pl.Buffered(1))
    res = pl.pallas_call(
        functools.partial(_inproj_kernel, has_y=has_y, kvh=kvh),
        grid=(b, s // tm),
        in_specs=[row] + ([row] if has_y else []) + [pl.BlockSpec((1, d), lambda bi, ti: (0, 0)),
                                                    const(w_main), const(w_nsa)],
        out_specs=out_specs,
        out_shape=out_shape,
        compiler_params=_cparams("parallel", "parallel"),
        name="inproj",
    )(*([x3, y3] if has_y else [x3]), gain, w_main, w_nsa)
    return tuple(res[:5]) + ((res[5],) if has_y else (x3,))


def _rnn_kernel(x_ref, xp_ref, y_ref, cw_ref, cb_ref, wa_ref, ba_ref, wx_ref, bx_ref, lam_ref,
                o_ref, h_scr):
    ti = pl.program_id(1)
    ts = x_ref.shape[0]
    kw = cw_ref.shape[0]

    @pl.when(ti == 0)
    def _():
        h_scr[...] = jnp.zeros_like(h_scr)

    x = x_ref[...].astype(F32)
    halo = jnp.where(ti == 0, 0.0, xp_ref[...].astype(F32))
    groups_x = jnp.concatenate([halo, x], axis=0).reshape(ts // SUBLANES + 1, SUBLANES, x.shape[1])
    row_in = lax.broadcasted_iota(jnp.int32, (ts // SUBLANES, SUBLANES, x.shape[1]), 1)
    u = cb_ref[...] + cw_ref[kw - 1:kw, :] * groups_x[1:]
    for shift in range(1, kw):
        rot = pltpu.roll(groups_x, shift, axis=1)
        u = u + cw_ref[kw - 1 - shift:kw - shift, :] * jnp.where(row_in >= shift, rot[1:], rot[:-1])
    u = u.reshape(ts, x.shape[1])
    ub = u.astype(BF16)
    r = jax.nn.sigmoid(_dot(ub, wa_ref[...]) + ba_ref[...])
    gi = jax.nn.sigmoid(_dot(ub, wx_ref[...]) + bx_ref[...])
    lam = lam_ref[...]
    softplus_neg = jnp.maximum(-lam, 0.0) + jnp.log(1.0 + jnp.exp(-jnp.abs(lam)))
    log_a = -RG_C * r * softplus_neg
    a = jnp.exp(log_a)
    b = jnp.sqrt(1.0 - jnp.exp(2.0 * log_a)) * (gi * u)

    a = a.reshape(ts // SUBLANES, SUBLANES, a.shape[1])
    b = b.reshape(a.shape)
    row_in_group = lax.broadcasted_iota(jnp.int32, a.shape, 1)
    d = 1
    while d < SUBLANES:
        keep = row_in_group >= d
        a_sh = pltpu.roll(a, d, axis=1)
        b_sh = pltpu.roll(b, d, axis=1)
        b = jnp.where(keep, a * b_sh + b, b)
        a = jnp.where(keep, a * a_sh, a)
        d *= 2
    h_prev = h_scr[...]
    groups = []
    for r in range(ts // SUBLANES):
        h_r = a[r] * h_prev + b[r]
        groups.append(h_r)
        h_prev = h_r[SUBLANES - 1:SUBLANES, :]
    h = jnp.concatenate(groups, axis=0)
    h_scr[...] = h_prev
    o_ref[...] = (_gelu_tanh(y_ref[...].astype(F32)) * h).astype(BF16)


def _rnn_branch(proj3, conv_w, conv_b, wa_bd, ba, wx_bd, bx, lam, ts=512):
    b, s, _ = proj3.shape
    w = conv_w.shape[1]
    nb = w // COL_BLOCK
    rows_per = ts // SUBLANES
    const = lambda shape: pl.BlockSpec(shape, lambda bi, ti: (0, 0))
    return pl.pallas_call(
        _rnn_kernel,
        grid=(b, s // ts),
        in_specs=[
            pl.BlockSpec((None, ts, w), lambda bi, ti: (bi, ti, 0)),
            pl.BlockSpec((None, SUBLANES, w), lambda bi, ti: (bi, jnp.maximum(ti * rows_per - 1, 0), 0)),
            pl.BlockSpec((None, ts, w), lambda bi, ti: (bi, ti, nb)),
            const(conv_w.shape), const((1, w)), const((w, w)), const((1, w)), const((w, w)), const((1, w)),
            const((1, w)),
        ],
        out_specs=pl.BlockSpec((None, ts, w), lambda bi, ti: (bi, ti, 0)),
        out_shape=jax.ShapeDtypeStruct((b, s, w), BF16),
        scratch_shapes=[pltpu.VMEM((1, w), F32)],
        compiler_params=_cparams("parallel", "arbitrary"),
        name="rnn_branch",
    )(proj3, proj3, proj3, conv_w, conv_b, wa_bd, ba, wx_bd, bx, lam)


def _kvprep_kernel(kcf_ref, vcf_ref, ks_ref, kw_ref, pe_ref, w1_ref, b1_ref, w2_ref, b2_ref, kn_ref,
                   kc_o, vc_o, ks_o, kw_o):
    half = kcf_ref.shape[1]
    nrow = kcf_ref.shape[0]

    def compress(flat_ref, i):
        xf = flat_ref[...].astype(F32)
        xa = (xf + pe_ref[i, 0:1, :]).astype(BF16)
        xb = (xf + pe_ref[i, 1:2, :]).astype(BF16)
        first = _dot(xa, w1_ref[i, 0:half, :])
        second = _dot(xb, w1_ref[i, half:2 * half, :])
        hid = first + pltpu.roll(second, nrow - 1, axis=0) + b1_ref[i]
        hid = _gelu_tanh(hid).astype(BF16)
        return _dot(hid, w2_ref[i]) + b2_ref[i]

    kc_o[...] = _rms(compress(kcf_ref, 0), kn_ref[0:1, :]).astype(BF16)
    vc_o[...] = compress(vcf_ref, 1).astype(BF16)
    ks = _rms(ks_ref[...].astype(F32), kn_ref[1:2, :])
    row = lax.broadcasted_iota(jnp.int32, ks.shape, 0)
    col = lax.broadcasted_iota(jnp.int32, ks.shape, 1)
    block_hot = jnp.where((row // SEL_LEN) % SUBLANES == col, 1.0, 0.0)
    ks_o[...] = jnp.concatenate([ks, block_hot], axis=1).astype(BF16)
    kw_o[...] = _rms(kw_ref[...].astype(F32), kn_ref[2:3, :]).astype(BF16)


def _kvprep(kc_flat, vc_flat, ks, kw, pe_flat, w1, b1, w2, b2, k_norm):
    b, g, nrow, half = kc_flat.shape
    s = ks.shape[2]
    dh = ks.shape[3]
    bg = lambda shape: pl.BlockSpec((None, None) + shape, lambda bi, gi: (bi, gi, 0, 0))
    full = lambda a: pl.BlockSpec(a.shape, lambda bi, gi: (0,) * a.ndim)
    return pl.pallas_call(
        _kvprep_kernel,
        grid=(b, g),
        in_specs=[bg((nrow, half)), bg((nrow, half)), bg((s, dh)), bg((s, dh)),
                  full(pe_flat), full(w1), full(b1), full(w2), full(b2), full(k_norm)],
        out_specs=[bg((nrow, dh)), bg((nrow, dh)), bg((s, 2 * dh)), bg((s, dh))],
        out_shape=[jax.ShapeDtypeStruct((b, g, nrow, dh), BF16), jax.ShapeDtypeStruct((b, g, nrow, dh), BF16),
                   jax.ShapeDtypeStruct((b, g, s, 2 * dh), BF16), jax.ShapeDtypeStruct((b, g, s, dh), BF16)],
        compiler_params=_cparams("parallel", "parallel"),
        name="kvprep",
    )(kc_flat, vc_flat, ks, kw, pe_flat, w1, b1, w2, b2, k_norm)


def _nsa_kernel(qt_ref, kc_ref, vct_ref, ks_ref, vst_ref, kw_ref, vwt_ref, gt_ref, qn_ref, ov_ref, cb_ref, bb_ref,
                o_ref, sel_scr, *, key_tile):
    c = pl.program_id(2)
    tq = Q_BLOCK
    t0 = c * tq
    n_sel = ov_ref.shape[0]
    ncp = kc_ref.shape[0]
    top_k = min(SEL_TOPK, n_sel)
    lanes = NSA_GROUP * tq
    blocks_per_tile = key_tile // SEL_LEN

    qf = qt_ref[...].astype(F32)
    qn = qf * lax.rsqrt(jnp.mean(qf * qf, axis=0, keepdims=True) + RMS_EPS) * qn_ref[...]
    qs = (qn * (HEAD_DIM ** -0.5 * LOG2E)).astype(BF16)
    t_lane = t0 + (lax.broadcasted_iota(jnp.int32, (1, lanes), 1) & (tq - 1))

    n_tiles = (t0 + tq + key_tile - 1) // key_tile
    n_full = n_tiles - 1

    pad_rows = ks_ref.shape[1] - HEAD_DIM - 2 * SUBLANES
    assert blocks_per_tile == SUBLANES and pad_rows >= 0

    def scores(kt, bias_rows=None):
        k0 = pl.multiple_of(kt * key_tile, key_tile)
        if bias_rows is None:
            bias_rows = jnp.zeros((SUBLANES, lanes), F32)
        extra = jnp.concatenate([bias_rows, jnp.zeros((SUBLANES, lanes), F32)], axis=0).astype(BF16)
        q_aug = jnp.concatenate([qs, extra, jnp.zeros((pad_rows, lanes), BF16)], axis=0)
        return _dot(ks_ref[pl.ds(k0, key_tile), :], q_aug)

    wl = WINDOW + tq
    w0 = pl.multiple_of(jnp.maximum(t0 - WINDOW, 0), tq)
    s_c = _dot(kc_ref[...], qs)
    s_w = _dot(kw_ref[pl.ds(w0, wl), :], qs)
    s_l = scores(n_full)

    off_c = pl.multiple_of(cb_ref.shape[0] - ncp - t0 // CMP_STRIDE, SUBLANES)
    s_c = s_c + cb_ref[pl.ds(off_c, ncp), :]
    e_c = jnp.exp2(s_c - jnp.max(s_c, axis=0, keepdims=True))
    inv_c = 1.0 / jnp.sum(e_c, axis=0, keepdims=True)
    p_c = e_c * jnp.where(t_lane >= CMP_LEN - 1, inv_c, 0.0)
    o_c = _dot(vct_ref[...], p_c.astype(BF16))

    j_i = lax.broadcasted_iota(jnp.int32, (n_sel, tq), 0)
    cur = (t0 + lax.broadcasted_iota(jnp.int32, (n_sel, tq), 1)) // SEL_LEN
    ps = p_c[:, 0:tq]
    for h in range(1, NSA_GROUP):
        ps = ps + p_c[:, h * tq:(h + 1) * tq]
    ps_hi = ps.astype(BF16)
    ps_lo = (ps - ps_hi.astype(F32)).astype(BF16)
    imp = _dot(ov_ref[...], ps_hi) + _dot(ov_ref[...], ps_lo)
    forced = (j_i == 0) | (j_i == cur) | (j_i == cur - 1)
    imp = jnp.where(forced, FORCE_SCORE, imp)
    imp = jnp.where(j_i <= cur, imp, -1.0)
    slabs = [imp[v * SUBLANES:(v + 1) * SUBLANES, :] for v in range(n_sel // SUBLANES)]
    j_slab = lax.broadcasted_iota(jnp.int32, (SUBLANES, tq), 0)
    cnts = [jnp.zeros((SUBLANES, tq), jnp.int32) for _ in slabs]
    for jp in range(n_sel):
        r = imp[jp:jp + 1, :]
        for v, slab in enumerate(slabs):
            lo = v * SUBLANES
            if lo > jp:
                before = r >= slab
            elif lo + SUBLANES - 1 <= jp:
                before = r > slab
            else:
                before = (r > slab) | ((r == slab) & (j_slab + lo > jp))
            cnts[v] = cnts[v] + before.astype(jnp.int32)
    cnt = jnp.concatenate(cnts, axis=0)
    sel_bias = jnp.where((cnt < top_k) & (imp >= 0.0), 0.0, NEG_INF)
    sel_scr[...] = jnp.concatenate([sel_bias] * NSA_GROUP, axis=1)

    off_w = pl.multiple_of(WINDOW - (t0 - w0), tq)
    s_w = s_w + bb_ref[pl.ds(off_w, wl), :]
    e_w = jnp.exp2(s_w - jnp.max(s_w, axis=0, keepdims=True))
    a_w = _dot(vwt_ref[:, pl.ds(w0, wl)], e_w.astype(BF16))
    o_w = a_w[0:HEAD_DIM] * (1.0 / a_w[HEAD_DIM:HEAD_DIM + 1])

    def bias_rows(kt):
        return sel_scr[pl.ds(pl.multiple_of(kt * blocks_per_tile, SUBLANES), blocks_per_tile), :]

    def update(carry, kt, s):
        m_o, a_o = carry
        k0 = pl.multiple_of(kt * key_tile, key_tile)
        m_n = jnp.maximum(m_o, jnp.max(s, axis=0, keepdims=True))
        p = jnp.exp2(s - m_n).astype(BF16)
        a_n = jnp.exp2(m_o - m_n) * a_o + _dot(vst_ref[:, pl.ds(k0, key_tile)], p)
        return m_n, a_n

    def init():
        return jnp.full((1, lanes), NEG_INF, F32), jnp.zeros((vst_ref.shape[0], lanes), F32)

    off_l = pl.multiple_of(WINDOW - (t0 - n_full * key_tile), tq)
    last_rows = bias_rows(n_full)
    last_bias = jnp.concatenate(
        [jnp.broadcast_to(last_rows[jb:jb + 1, :], (SEL_LEN, lanes)) for jb in range(blocks_per_tile)], axis=0)
    c_first = update(init(), n_full, s_l + last_bias + bb_ref[pl.ds(off_l, key_tile), :])

    def pair_step(i, carry):
        c_a, c_b = carry
        s_a = scores(2 * i, bias_rows(2 * i))
        s_b = scores(2 * i + 1, jnp.where(2 * i + 1 < n_full, bias_rows(2 * i + 1), NEG_INF))
        return update(c_a, 2 * i, s_a), update(c_b, 2 * i + 1, s_b)

    (m_a, a_a), (m_b, a_b) = lax.fori_loop(0, (n_full + 1) // 2, pair_step, (c_first, init()))
    m_ab = jnp.maximum(m_a, m_b)
    a_s = jnp.exp2(m_a - m_ab) * a_a + jnp.exp2(m_b - m_ab) * a_b
    o_s = a_s[0:HEAD_DIM] * (1.0 / a_s[HEAD_DIM:HEAD_DIM + 1])

    gates = jax.nn.sigmoid(gt_ref[...].astype(F32))
    for h in range(NSA_GROUP):
        hs = slice(h * tq, (h + 1) * tq)
        gb = h * N_NSA_BRANCHES
        o = (gates[gb:gb + 1, :] * o_c[:, hs] + gates[gb + 1:gb + 2, :] * o_s[:, hs]
             + gates[gb + 2:gb + 3, :] * o_w[:, hs])
        o_ref[:, h * HEAD_DIM:(h + 1) * HEAD_DIM] = o.T.astype(BF16)


def _band_bias(n_rows, lanes):
    i = jnp.arange(n_rows)[:, None]
    tl = (jnp.arange(lanes) % Q_BLOCK)[None, :]
    return jnp.where((tl < i) & (i <= tl + WINDOW), 0.0, NEG_INF).astype(F32)


def _cmp_bias(n_lead, ncp, lanes):
    i = jnp.arange(n_lead + ncp)[:, None]
    tl = (jnp.arange(lanes) % Q_BLOCK)[None, :]
    return jnp.where(CMP_STRIDE * (i - n_lead) + CMP_LEN - 1 <= tl, 0.0, NEG_INF).astype(F32)


def _nsa(qt5, k_cmp, v_cmp_t, k_sel, v_sel_t, k_win, v_win_t, gates_t, q_norm_col, overlap_t, key_tile=512):
    b, g, nchunk, dh, lanes = qt5.shape
    s = k_sel.shape[2]
    ncp = k_cmp.shape[2]
    ng = gates_t.shape[2]
    n_sel = overlap_t.shape[0]
    assert key_tile <= WINDOW and key_tile % Q_BLOCK == 0 and s % key_tile == 0
    cmp_bias = _cmp_bias((s - Q_BLOCK) // CMP_STRIDE, ncp, lanes)
    band_bias = _band_bias(2 * WINDOW + Q_BLOCK, lanes)
    bg = lambda shape: pl.BlockSpec((None, None) + shape, lambda bi, gi, ci: (bi, gi, 0, 0))
    const = lambda a: pl.BlockSpec(a.shape, lambda bi, gi, ci: (0, 0), pipeline_mode=pl.Buffered(1))
    return pl.pallas_call(
        functools.partial(_nsa_kernel, key_tile=key_tile),
        grid=(b, g, nchunk),
        in_specs=[
            pl.BlockSpec((None, None, None, dh, lanes), lambda bi, gi, ci: (bi, gi, ci, 0, 0)),
            bg((ncp, dh)), bg((dh, ncp)), bg((s, k_sel.shape[3])), bg((v_sel_t.shape[2], s)), bg((s, dh)),
            bg((v_win_t.shape[2], s)),
            pl.BlockSpec((None, None, ng, Q_BLOCK), lambda bi, gi, ci: (bi, gi, 0, ci)),
            pl.BlockSpec(q_norm_col.shape, lambda bi, gi, ci: (0, 0)),
            pl.BlockSpec(overlap_t.shape, lambda bi, gi, ci: (0, 0)),
            const(cmp_bias), const(band_bias),
        ],
        out_specs=pl.BlockSpec((None, Q_BLOCK, NSA_GROUP * dh), lambda bi, gi, ci: (bi, ci, gi)),
        out_shape=jax.ShapeDtypeStruct((b, s, g * NSA_GROUP * dh), BF16),
        scratch_shapes=[pltpu.VMEM((n_sel, lanes), F32)],
        compiler_params=_cparams("parallel", "parallel", "arbitrary"),
        name="nsa",
    )(qt5, k_cmp, v_cmp_t, k_sel, v_sel_t, k_win, v_win_t, gates_t, q_norm_col, overlap_t, cmp_bias, band_bias)


def _merge_kernel(x_ref, yr_ref, yn_ref, sb_ref, sc_ref, sh_ref, scp_ref, shp_ref, g0_ref, g1_ref, g2_ref,
                  cw_ref, wr_ref, wn_ref, ws_ref, wo_ref, o_ref):
    ti = pl.program_id(1)
    tm = x_ref.shape[0]
    kw = cw_ref.shape[0]
    prod = sc_ref[...].astype(F32) * sh_ref[...].astype(F32)
    halo = jnp.where(ti == 0, 0.0, scp_ref[...].astype(F32) * shp_ref[...].astype(F32))
    groups_p = jnp.concatenate([halo, prod], axis=0).reshape(tm // SUBLANES + 1, SUBLANES, prod.shape[1])
    row_in = lax.broadcasted_iota(jnp.int32, (tm // SUBLANES, SUBLANES, prod.shape[1]), 1)
    conv = cw_ref[kw - 1:kw, :] * groups_p[1:]
    for shift in range(1, kw):
        rot = pltpu.roll(groups_p, shift, axis=1)
        conv = conv + cw_ref[kw - 1 - shift:kw - shift, :] * jnp.where(row_in >= shift, rot[1:], rot[:-1])
    y_sc = (sb_ref[...].astype(F32) * conv.reshape(tm, prod.shape[1])).astype(BF16)
    merged = (jax.nn.sigmoid(g0_ref[...].astype(F32)) * _dot(yr_ref[...], wr_ref[...])
              + jax.nn.sigmoid(g1_ref[...].astype(F32)) * _dot(yn_ref[...], wn_ref[...])
              + jax.nn.sigmoid(g2_ref[...].astype(F32)) * _dot(y_sc, ws_ref[...]))
    o_ref[...] = x_ref[...] + _dot(merged.astype(BF16), wo_ref[...])


def _merge(x3, proj3, y_rnn, y_nsa, sc_conv_w, w_rnn_out, w_nsa_out, w_sc_out, w_out, sc_block, gate_block, tm=512):
    b, s, d = x3.shape
    w = y_rnn.shape[2]
    rows_per = tm // SUBLANES
    tile = lambda width, col: pl.BlockSpec((None, tm, width), lambda bi, ti: (bi, ti, col))
    prev = lambda col: pl.BlockSpec((None, SUBLANES, w),
                                    lambda bi, ti: (bi, jnp.maximum(ti * rows_per - 1, 0), col))
    const = lambda a: pl.BlockSpec(a.shape, lambda bi, ti: (0, 0))
    return pl.pallas_call(
        _merge_kernel,
        grid=(b, s // tm),
        in_specs=[
            tile(d, 0), tile(w, 0), tile(w, 0),
            tile(w, sc_block), tile(w, sc_block + 1), tile(w, sc_block + 2),
            prev(sc_block + 1), prev(sc_block + 2),
            tile(d, gate_block), tile(d, gate_block + 1), tile(d, gate_block + 2),
            const(sc_conv_w), const(w_rnn_out), const(w_nsa_out), const(w_sc_out), const(w_out),
        ],
        out_specs=tile(d, 0),
        out_shape=jax.ShapeDtypeStruct((b, s, d), F32),
        compiler_params=_cparams("parallel", "parallel"),
        name="merge",
    )(x3, y_rnn, y_nsa, proj3, proj3, proj3, proj3, proj3, proj3, proj3, proj3,
      sc_conv_w, w_rnn_out, w_nsa_out, w_sc_out, w_out)


MOE_BLOCK = 1024
MOE_ROW_TILE = 304
SLOT_IND = EXPERTS_PER_GROUP
SLOT_POS = EXPERTS_PER_GROUP + 1


ROUTER_ROWS = 48


def _route_t(logits, n_experts, n_groups):
    row_i = lax.broadcasted_iota(jnp.int32, logits.shape, 0)
    row = row_i.astype(F32)
    big = float(logits.shape[0])
    is_g = (row_i >= n_experts) & (row_i < n_experts + n_groups)
    gl = jnp.where(is_g, logits, NEG_INF)
    gmax = jnp.max(gl, axis=0, keepdims=True)
    grp = jnp.min(jnp.where(is_g & (gl == gmax), row - n_experts, big), axis=0, keepdims=True)
    p_grp = 1.0 / jnp.sum(jnp.where(is_g, jnp.exp(gl - gmax), 0.0), axis=0, keepdims=True)
    in_g = (row_i < n_experts) & ((row_i // EXPERTS_PER_GROUP).astype(F32) == grp)
    el = jnp.where(in_g, logits, NEG_INF)
    ee = jnp.where(in_g, jnp.exp(el - jnp.max(el, axis=0, keepdims=True)), 0.0)
    ep = ee / jnp.sum(ee, axis=0, keepdims=True)
    top1 = jnp.max(jnp.where(in_g, ep, -1.0), axis=0, keepdims=True)
    i1 = jnp.min(jnp.where(in_g & (ep == top1), row, big), axis=0, keepdims=True)
    rest = jnp.where(in_g & (row != i1), ep, -1.0)
    top2 = jnp.max(rest, axis=0, keepdims=True)
    i2 = jnp.min(jnp.where(rest == top2, row, big), axis=0, keepdims=True)
    denom = top1 + top2
    cw = (jnp.where(row == i1, p_grp * top1 / denom, 0.0)
          + jnp.where(row == i2, p_grp * top2 / denom, 0.0))
    return cw, grp


def _router_kernel(x_ref, g_ref, wr_ref, br_ref, tri_ref, xn_ref, slot_ref, info_t_ref, *, n_experts, n_groups):
    xn = _rms(x_ref[...], g_ref[...]).astype(BF16)
    xn_ref[...] = xn
    tb = xn.shape[0]
    logits = lax.dot_general(wr_ref[...], xn, (((1,), (1,)), ((), ())), preferred_element_type=F32) + br_ref[...]
    cw, grp = _route_t(logits, n_experts, n_groups)
    g_row = lax.broadcasted_iota(jnp.int32, (SUBLANES, tb), 0)
    ind = jnp.where((g_row.astype(F32) == grp) & (g_row < n_groups), 1.0, 0.0)
    pos = _dot(ind.astype(BF16), tri_ref[...])
    info_t_ref[...] = jnp.where(g_row < n_groups, ind, pltpu.roll(pos, n_groups, axis=0))
    pad = jnp.zeros((LANES - 2 * SUBLANES, tb), F32)
    extra_row = lax.broadcasted_iota(jnp.int32, (SUBLANES, tb), 0)
    for g in range(n_groups):
        w_g = cw[g * EXPERTS_PER_GROUP:(g + 1) * EXPERTS_PER_GROUP]
        extra = jnp.where(extra_row == SLOT_IND - EXPERTS_PER_GROUP, ind[g:g + 1],
                          jnp.where(extra_row == SLOT_POS - EXPERTS_PER_GROUP, pos[g:g + 1], 0.0))
        slot_ref[:, g * LANES:(g + 1) * LANES] = jnp.concatenate([w_g, extra, pad], axis=0).T


def _router(x2, gain, w_router_t, b_router_t, n_experts, n_groups):
    n, d = x2.shape
    tb = MOE_BLOCK
    assert EXPERTS_PER_GROUP == SUBLANES and n_groups <= SUBLANES // 2
    tri = (jnp.arange(tb)[:, None] < jnp.arange(tb)[None, :]).astype(BF16)
    const = lambda a: pl.BlockSpec(a.shape, lambda i: (0, 0))
    return pl.pallas_call(
        functools.partial(_router_kernel, n_experts=n_experts, n_groups=n_groups),
        grid=(n // tb,),
        in_specs=[pl.BlockSpec((tb, d), lambda i: (i, 0)), const(gain), const(w_router_t), const(b_router_t),
                  pl.BlockSpec(tri.shape, lambda i: (0, 0), pipeline_mode=pl.Buffered(1))],
        out_specs=[pl.BlockSpec((tb, d), lambda i: (i, 0)),
                   pl.BlockSpec((tb, n_groups * LANES), lambda i: (i, 0)),
                   pl.BlockSpec((2 * n_groups, tb), lambda i: (0, i))],
        out_shape=[jax.ShapeDtypeStruct((n, d), BF16),
                   jax.ShapeDtypeStruct((n, n_groups * LANES), F32),
                   jax.ShapeDtypeStruct((2 * n_groups, n), F32)],
        compiler_params=_cparams("parallel"),
        name="router",
    )(x2, gain, w_router_t, b_router_t, tri)


def _moe_kernel(xn_ref, slot_ref, info_t_ref, wg_ref, wu_ref, wd_ref, o_ref, xc_scr, yc_scr, wc_scr, *, n_groups):
    g = pl.program_id(1)
    tb = xn_ref.shape[0]
    rt = MOE_ROW_TILE

    @pl.when(g == 0)
    def _():
        o_ref[...] = jnp.zeros_like(o_ref)

    slot = slot_ref[...]
    ind_col = slot[:, SLOT_IND:SLOT_IND + 1] > 0.5
    pos_col = slot[:, SLOT_POS:SLOT_POS + 1]
    ind_row = info_t_ref[pl.ds(g, 1), :]
    pos_row = info_t_ref[pl.ds(n_groups + g, 1), :]
    n_rows = jnp.sum(ind_row).astype(jnp.int32)
    n_tiles = (n_rows + rt - 1) // rt
    slot_hi = slot.astype(BF16)
    slot_lo = (slot - slot_hi.astype(F32)).astype(BF16)
    row_id = lax.broadcasted_iota(jnp.int32, (rt, tb), 0).astype(F32)
    col_id = lax.broadcasted_iota(jnp.int32, (tb, rt), 1).astype(F32)

    def rows(t):
        return pl.ds(pl.multiple_of(t * rt, 2 * SUBLANES), rt)

    def gather(t, _):
        base = (t * rt).astype(F32)
        pick = jnp.where((pos_row == row_id + base) & (ind_row > 0.5), 1.0, 0.0).astype(BF16)
        xc_scr[rows(t), :] = _dot(pick, xn_ref[...]).astype(BF16)
        wc_scr[rows(t), :] = _dot(pick, slot_hi) + _dot(pick, slot_lo)
        yc_scr[rows(t), :] = jnp.zeros((rt, yc_scr.shape[1]), F32)
        return 0

    lax.fori_loop(0, n_tiles, gather, 0)

    for e in range(EXPERTS_PER_GROUP):
        def expert(t, _, e=e):
            xg = xc_scr[rows(t), :]
            h = jax.nn.silu(_dot(xg, wg_ref[e])) * _dot(xg, wu_ref[e]) * wc_scr[rows(t), :][:, e:e + 1]
            yc_scr[rows(t), :] += _dot(h.astype(BF16), wd_ref[e])
            return 0

        lax.fori_loop(0, n_tiles, expert, 0)

    def scatter(t, _):
        base = (t * rt).astype(F32)
        put = jnp.where((pos_col == col_id + base) & ind_col, 1.0, 0.0).astype(BF16)
        o_ref[...] += _dot(put, yc_scr[rows(t), :].astype(BF16)).astype(BF16)
        return 0

    lax.fori_loop(0, n_tiles, scatter, 0)


def _moe(xn, slots, info_t, w_gate, w_up, w_down, n_groups):
    n, d = xn.shape
    f = w_gate.shape[2]
    tb = MOE_BLOCK
    epg = EXPERTS_PER_GROUP
    assert MOE_ROW_TILE % (2 * SUBLANES) == 0
    cap = -(-tb // MOE_ROW_TILE) * MOE_ROW_TILE
    return pl.pallas_call(
        functools.partial(_moe_kernel, n_groups=n_groups),
        grid=(n // tb, n_groups),
        in_specs=[
            pl.BlockSpec((tb, d), lambda i, g: (i, 0)),
            pl.BlockSpec((tb, LANES), lambda i, g: (i, g)),
            pl.BlockSpec((2 * n_groups, tb), lambda i, g: (0, i)),
            pl.BlockSpec((epg, d, f), lambda i, g: (g, 0, 0)),
            pl.BlockSpec((epg, d, f), lambda i, g: (g, 0, 0)),
            pl.BlockSpec((epg, f, d), lambda i, g: (g, 0, 0)),
        ],
        out_specs=pl.BlockSpec((tb, d), lambda i, g: (i, 0)),
        out_shape=jax.ShapeDtypeStruct((n, d), BF16),
        scratch_shapes=[pltpu.VMEM((cap, d), BF16), pltpu.VMEM((cap, d), F32), pltpu.VMEM((cap, LANES), F32)],
        compiler_params=_cparams("parallel", "arbitrary"),
        name="moe",
    )(xn, slots, info_t, w_gate, w_up, w_down)


def _residual_kernel(x_ref, y_ref, o_ref):
    o_ref[...] = x_ref[...] + y_ref[...].astype(F32)


def _residual(x2, y2, tm=1024):
    n, d = x2.shape
    spec = pl.BlockSpec((tm, d), lambda i: (i, 0))
    return pl.pallas_call(
        _residual_kernel, grid=(n // tm,), in_specs=[spec, spec], out_specs=spec,
        out_shape=jax.ShapeDtypeStruct((n, d), F32), compiler_params=_cparams("parallel"), name="residual",
    )(x2, y2)


def _block_diag(w):
    nb, bs, _ = w.shape
    eye = jnp.eye(nb, dtype=w.dtype)
    return (eye[:, None, :, None] * w[:, :, None, :]).reshape(nb * bs, nb * bs)


def _overlap_t(n_sel, ncp):
    n = jnp.arange(ncp)
    j = jnp.arange(n_sel)
    c0 = n * CMP_STRIDE
    s0 = j * SEL_LEN
    ov = jnp.clip(jnp.minimum(c0[None, :] + CMP_LEN, s0[:, None] + SEL_LEN)
                  - jnp.maximum(c0[None, :], s0[:, None]), 0).astype(F32) / CMP_LEN
    return ov.astype(BF16)


def kernel(x, mix_norm, w_in, rnn_conv_w, rnn_conv_b, rg_wa, rg_ba, rg_wx, rg_bx, rg_lambda, cmp_pe, cmp_w1, cmp_b1, cmp_w2, cmp_b2, q_norm, k_norm, sc_conv_w, w_rnn_out, w_nsa_out, w_sc_out, w_out, ffn_norm, router_group_w, router_group_b, router_expert_w, router_expert_b, exp_w_gate, exp_w_up, exp_w_down):
    b, s, d = x.shape
    depth = w_in.shape[0]
    rnn_w = rnn_conv_w.shape[2]
    sc_w = sc_conv_w.shape[2]
    nsa_w = w_nsa_out.shape[1]
    dh = q_norm.shape[1]
    n_heads = nsa_w // dh
    kvh = n_heads // NSA_GROUP
    kv_w = kvh * dh
    n_groups = router_group_w.shape[2]
    n_experts = router_expert_w.shape[2]
    n = b * s
    assert rnn_w == COL_BLOCK and sc_w == COL_BLOCK and nsa_w == COL_BLOCK and dh == HEAD_DIM
    assert 4 * kv_w == COL_BLOCK and d % COL_BLOCK == 0
    assert s % COL_BLOCK == 0 and s >= WINDOW + Q_BLOCK and s % SEL_LEN == 0
    assert n_experts + n_groups <= ROUTER_ROWS and n_experts == n_groups * EXPERTS_PER_GROUP

    nrow = s // CMP_STRIDE
    n_sel = s // SEL_LEN
    n_gate = n_heads * N_NSA_BRANCHES
    nsa_cols = nsa_w + 6 * kv_w + n_gate
    c_q = 2 * COL_BLOCK
    c_sc = c_q + nsa_cols
    c_mg = c_sc + 3 * COL_BLOCK
    gate_block = 2 * COL_BLOCK // d
    sc_block = (2 * COL_BLOCK + w_in.shape[2] - c_mg) // COL_BLOCK
    overlap_t = _overlap_t(n_sel, nrow)
    chunk = CMP_STRIDE * dh

    w_main, w_nsa = _pack_w_in(w_in, c_q, c_sc, c_mg)
    y_moe = None
    for l in range(depth):
        proj3, qt5, kv_rows, v_t, gates_t, x = _inproj(
            x, None if y_moe is None else y_moe.reshape(b, s, d), mix_norm[l][None, :], w_main[l], w_nsa[l], kvh)

        y_rnn = _rnn_branch(
            proj3, rnn_conv_w[l], rnn_conv_b[l][None, :],
            _block_diag(rg_wa[l]).astype(BF16), rg_ba[l].reshape(1, rnn_w),
            _block_diag(rg_wx[l]).astype(BF16), rg_bx[l].reshape(1, rnn_w), rg_lambda[l][None, :])

        kc_flat = kv_rows[0].reshape(b, kvh, nrow, chunk)
        vc_flat = kv_rows[1].reshape(b, kvh, nrow, chunk)
        pe_flat = cmp_pe[l].reshape(2, 2, chunk)
        k_cmp, v_cmp, k_sel, k_win = _kvprep(
            kc_flat, vc_flat, kv_rows[2], kv_rows[3], pe_flat, cmp_w1[l].astype(BF16), cmp_b1[l][:, None, :],
            cmp_w2[l].astype(BF16), cmp_b2[l][:, None, :], k_norm[l])
        y_nsa = _nsa(qt5, k_cmp, v_cmp.transpose(0, 1, 3, 2), k_sel, v_t[0], k_win, v_t[1], gates_t,
                     q_norm[l][:, None], overlap_t)

        x = _merge(x, proj3, y_rnn, y_nsa, sc_conv_w[l], w_rnn_out[l].astype(BF16), w_nsa_out[l].astype(BF16),
                   w_sc_out[l].astype(BF16), w_out[l].astype(BF16), sc_block, gate_block)

        n_pad = ROUTER_ROWS - n_experts - n_groups
        w_router_t = jnp.concatenate(
            [router_expert_w[l].T, router_group_w[l].T, jnp.zeros((n_pad, d), F32)], axis=0).astype(BF16)
        b_router_t = jnp.concatenate([router_expert_b[l], router_group_b[l], jnp.zeros((n_pad,), F32)])[:, None]
        xn, slots, info_t = _router(x.reshape(n, d), ffn_norm[l][None, :], w_router_t, b_router_t, n_experts, n_groups)
        y_moe = _moe(xn, slots, info_t, exp_w_gate[l].astype(BF16), exp_w_up[l].astype(BF16),
                     exp_w_down[l].astype(BF16), n_groups)
    return _residual(x.reshape(n, d), y_moe).reshape(b, s, d)
```

```python
import functools

import jax
import jax.numpy as jnp
from jax import lax
from jax.experimental import pallas as pl
from jax.experimental.pallas import tpu as pltpu

F32 = jnp.float32
BF16 = jnp.bfloat16

RMS_EPS = 1e-6
NEG_INF = -1e30
FORCE_SCORE = 1e6
RG_C = 8.0
HEAD_DIM = 64
NSA_GROUP = 4
N_NSA_BRANCHES = 3
CMP_LEN = 32
CMP_STRIDE = 16
SEL_LEN = 64
SEL_TOPK = 16
WINDOW = 512
Q_BLOCK = 256
EXPERTS_PER_GROUP = 8
TOPK_IN_GROUP = 2

LANES = 128
SUBLANES = 8
COL_BLOCK = 512
VMEM_LIMIT = 56 * 1024 * 1024
LOG2E = 1.4426950408889634


def _cparams(*sem):
    return pltpu.CompilerParams(dimension_semantics=sem, vmem_limit_bytes=VMEM_LIMIT)


def _rms(x, gain):
    return x * lax.rsqrt(jnp.mean(x * x, axis=-1, keepdims=True) + RMS_EPS) * gain


def _gelu_tanh(x):
    return 0.5 * x * (1.0 + jnp.tanh(0.7978845608028654 * (x + 0.044715 * x * x * x)))


def _dot(a, b):
    return jnp.dot(a, b, preferred_element_type=F32)


def _pack_kernel(w_ref, main_ref, nsa_ref, *, c_q, c_sc, c_mg):
    w = w_ref[...]
    n_mg = w.shape[1] - c_mg
    main_ref[:, 0:c_q] = w[:, 0:c_q].astype(BF16)
    main_ref[:, c_q:c_q + n_mg] = w[:, c_mg:].astype(BF16)
    main_ref[:, c_q + n_mg:] = w[:, c_sc:c_mg].astype(BF16)
    nsa_ref[:, 0:c_sc - c_q] = w[:, c_q:c_sc].astype(BF16)
    nsa_ref[:, c_sc - c_q:] = jnp.zeros((w.shape[0], nsa_ref.shape[1] - (c_sc - c_q)), BF16)


def _pack_w_in(w_in, c_q, c_sc, c_mg, tr=128):
    depth, d, win = w_in.shape
    w_main = win - (c_sc - c_q)
    w_nsa = -(-(c_sc - c_q) // LANES) * LANES
    return pl.pallas_call(
        functools.partial(_pack_kernel, c_q=c_q, c_sc=c_sc, c_mg=c_mg),
        grid=(depth, d // tr),
        in_specs=[pl.BlockSpec((None, tr, win), lambda l, i: (l, i, 0))],
        out_specs=[pl.BlockSpec((None, tr, w_main), lambda l, i: (l, i, 0)),
                   pl.BlockSpec((None, tr, w_nsa), lambda l, i: (l, i, 0))],
        out_shape=[jax.ShapeDtypeStruct((depth, d, w_main), BF16), jax.ShapeDtypeStruct((depth, d, w_nsa), BF16)],
        compiler_params=_cparams("parallel", "parallel"),
        name="pack_w_in",
    )(w_in)


def _inproj_kernel(*refs, has_y, kvh):
    if has_y:
        x_ref, y_ref, g_ref, w_ref, wn_ref, o_ref, qt_ref, kr_ref, vt_ref, gt_ref, xo_ref = refs
        x = x_ref[...] + y_ref[...].astype(F32)
        xo_ref[...] = x
    else:
        x_ref, g_ref, w_ref, wn_ref, o_ref, qt_ref, kr_ref, vt_ref, gt_ref = refs
        x = x_ref[...]
    xn = _rms(x, g_ref[...]).astype(BF16)
    for c in range(w_ref.shape[1] // COL_BLOCK):
        sl = slice(c * COL_BLOCK, (c + 1) * COL_BLOCK)
        o_ref[:, sl] = _dot(xn, w_ref[:, sl]).astype(BF16)

    tm = x.shape[0]
    dh = HEAD_DIM
    tq = Q_BLOCK
    nsa = _dot(xn, wn_ref[...])
    for pair in range(kvh * NSA_GROUP // 2):
        slab_t = nsa[:, pair * LANES:(pair + 1) * LANES].T
        for half in range(2):
            head = 2 * pair + half
            g, h = head // NSA_GROUP, head % NSA_GROUP
            for j in range(tm // tq):
                qt_ref[g, j, :, h * tq:(h + 1) * tq] = slab_t[half * dh:(half + 1) * dh,
                                                             j * tq:(j + 1) * tq].astype(BF16)
    kv0 = kvh * NSA_GROUP * dh
    ones_rows = jnp.where(lax.broadcasted_iota(jnp.int32, (2 * SUBLANES, tm), 0) == 0, 1.0, 0.0).astype(BF16)
    for slab, (row_slot, col_slot) in enumerate(((0, None), (1, None), (2, None), (None, 0), (3, None), (None, 1))):
        v = nsa[:, kv0 + slab * kvh * dh:kv0 + (slab + 1) * kvh * dh]
        if row_slot is not None:
            for g in range(kvh):
                kr_ref[row_slot, g] = v[:, g * dh:(g + 1) * dh].astype(BF16)
        else:
            v_t = v.T
            for g in range(kvh):
                vt_ref[col_slot, g, 0:dh, :] = v_t[g * dh:(g + 1) * dh].astype(BF16)
                vt_ref[col_slot, g, dh:dh + 2 * SUBLANES, :] = ones_rows
    g0 = kv0 + 6 * kvh * dh
    gates_t = nsa[:, g0:g0 + LANES].T
    per_group = NSA_GROUP * N_NSA_BRANCHES
    for g in range(kvh):
        gt_ref[g] = gates_t[g * per_group:(g + 1) * per_group].astype(BF16)


def _inproj(x3, y3, gain, w_main, w_nsa, layer, kvh, tm=512):
    b, s, d = x3.shape
    wp = w_main.shape[2]
    dh = HEAD_DIM
    row = pl.BlockSpec((None, tm, d), lambda bi, ti: (bi, ti, 0))
    has_y = y3 is not None
    lanes = NSA_GROUP * Q_BLOCK
    per_group = NSA_GROUP * N_NSA_BRANCHES
    out_specs = [
        pl.BlockSpec((None, tm, wp), lambda bi, ti: (bi, ti, 0)),
        pl.BlockSpec((None, kvh, tm // Q_BLOCK, dh, lanes), lambda bi, ti: (bi, 0, ti, 0, 0)),
        pl.BlockSpec((4, None, kvh, tm, dh), lambda bi, ti: (0, bi, 0, ti, 0)),
        pl.BlockSpec((2, None, kvh, dh + 2 * SUBLANES, tm), lambda bi, ti: (0, bi, 0, 0, ti)),
        pl.BlockSpec((None, kvh, per_group, tm), lambda bi, ti: (bi, 0, 0, ti)),
    ] + ([row] if has_y else [])
    out_shape = [
        jax.ShapeDtypeStruct((b, s, wp), BF16),
        jax.ShapeDtypeStruct((b, kvh, s // Q_BLOCK, dh, lanes), BF16),
        jax.ShapeDtypeStruct((4, b, kvh, s, dh), BF16),
        jax.ShapeDtypeStruct((2, b, kvh, dh + 2 * SUBLANES, s), BF16),
        jax.ShapeDtypeStruct((b, kvh, per_group, s), BF16),
    ] + ([jax.ShapeDtypeStruct((b, s, d), F32)] if has_y else [])
    const = lambda a: pl.BlockSpec((None,) + a.shape[1:], lambda bi, ti: (layer, 0, 0), pipeline_mode=pl.Buffered(1))
    res = pl.pallas_call(
        functools.partial(_inproj_kernel, has_y=has_y, kvh=kvh),
        grid=(b, s // tm),
        in_specs=[row] + ([row] if has_y else []) + [pl.BlockSpec((1, d), lambda bi, ti: (0, 0)),
                                                    const(w_main), const(w_nsa)],
        out_specs=out_specs,
        out_shape=out_shape,
        compiler_params=_cparams("parallel", "parallel"),
        name="inproj",
    )(*([x3, y3] if has_y else [x3]), gain, w_main, w_nsa)
    return tuple(res[:5]) + ((res[5],) if has_y else (x3,))


def _rnn_kernel(x_ref, xp_ref, y_ref, cw_ref, cb_ref, wa_ref, ba_ref, wx_ref, bx_ref, lam_ref,
                o_ref, h_scr):
    ti = pl.program_id(1)
    ts = x_ref.shape[0]
    kw = cw_ref.shape[0]

    @pl.when(ti == 0)
    def _():
        h_scr[...] = jnp.zeros_like(h_scr)

    x = x_ref[...].astype(F32)
    halo = jnp.where(ti == 0, 0.0, xp_ref[...].astype(F32))
    groups_x = jnp.concatenate([halo, x], axis=0).reshape(ts // SUBLANES + 1, SUBLANES, x.shape[1])
    row_in = lax.broadcasted_iota(jnp.int32, (ts // SUBLANES, SUBLANES, x.shape[1]), 1)
    u = cb_ref[...] + cw_ref[kw - 1:kw, :] * groups_x[1:]
    for shift in range(1, kw):
        rot = pltpu.roll(groups_x, shift, axis=1)
        u = u + cw_ref[kw - 1 - shift:kw - shift, :] * jnp.where(row_in >= shift, rot[1:], rot[:-1])
    u = u.reshape(ts, x.shape[1])
    ub = u.astype(BF16)
    r = jax.nn.sigmoid(_dot(ub, wa_ref[...]) + ba_ref[...])
    gi = jax.nn.sigmoid(_dot(ub, wx_ref[...]) + bx_ref[...])
    lam = lam_ref[...]
    softplus_neg = jnp.maximum(-lam, 0.0) + jnp.log(1.0 + jnp.exp(-jnp.abs(lam)))
    log_a = -RG_C * r * softplus_neg
    a = jnp.exp(log_a)
    b = jnp.sqrt(1.0 - jnp.exp(2.0 * log_a)) * (gi * u)

    a = a.reshape(ts // SUBLANES, SUBLANES, a.shape[1])
    b = b.reshape(a.shape)
    row_in_group = lax.broadcasted_iota(jnp.int32, a.shape, 1)
    d = 1
    while d < SUBLANES:
        keep = row_in_group >= d
        a_sh = pltpu.roll(a, d, axis=1)
        b_sh = pltpu.roll(b, d, axis=1)
        b = jnp.where(keep, a * b_sh + b, b)
        a = jnp.where(keep, a * a_sh, a)
        d *= 2
    h_prev = h_scr[...]
    groups = []
    for r in range(ts // SUBLANES):
        h_r = a[r] * h_prev + b[r]
        groups.append(h_r)
        h_prev = h_r[SUBLANES - 1:SUBLANES, :]
    h = jnp.concatenate(groups, axis=0)
    h_scr[...] = h_prev
    o_ref[...] = (_gelu_tanh(y_ref[...].astype(F32)) * h).astype(BF16)


def _rnn_branch(proj3, conv_w, conv_b, wa_bd, ba, wx_bd, bx, lam, ts=512):
    b, s, _ = proj3.shape
    w = conv_w.shape[1]
    nb = w // COL_BLOCK
    rows_per = ts // SUBLANES
    const = lambda shape: pl.BlockSpec(shape, lambda bi, ti: (0, 0))
    return pl.pallas_call(
        _rnn_kernel,
        grid=(b, s // ts),
        in_specs=[
            pl.BlockSpec((None, ts, w), lambda bi, ti: (bi, ti, 0)),
            pl.BlockSpec((None, SUBLANES, w), lambda bi, ti: (bi, jnp.maximum(ti * rows_per - 1, 0), 0)),
            pl.BlockSpec((None, ts, w), lambda bi, ti: (bi, ti, nb)),
            const(conv_w.shape), const((1, w)), const((w, w)), const((1, w)), const((w, w)), const((1, w)),
            const((1, w)),
        ],
        out_specs=pl.BlockSpec((None, ts, w), lambda bi, ti: (bi, ti, 0)),
        out_shape=jax.ShapeDtypeStruct((b, s, w), BF16),
        scratch_shapes=[pltpu.VMEM((1, w), F32)],
        compiler_params=_cparams("parallel", "arbitrary"),
        name="rnn_branch",
    )(proj3, proj3, proj3, conv_w, conv_b, wa_bd, ba, wx_bd, bx, lam)


def _kvprep_kernel(kcf_ref, vcf_ref, ks_ref, kw_ref, pe_ref, w1_ref, b1_ref, w2_ref, b2_ref, kn_ref,
                   kc_o, vc_o, ks_o, kw_o):
    half = kcf_ref.shape[1]
    nrow = kcf_ref.shape[0]

    def compress(flat_ref, i):
        xf = flat_ref[...].astype(F32)
        xa = (xf + pe_ref[i, 0:1, :]).astype(BF16)
        xb = (xf + pe_ref[i, 1:2, :]).astype(BF16)
        first = _dot(xa, w1_ref[i, 0:half, :])
        second = _dot(xb, w1_ref[i, half:2 * half, :])
        hid = first + pltpu.roll(second, nrow - 1, axis=0) + b1_ref[i]
        hid = _gelu_tanh(hid).astype(BF16)
        return _dot(hid, w2_ref[i]) + b2_ref[i]

    kc_o[...] = _rms(compress(kcf_ref, 0), kn_ref[0:1, :]).astype(BF16)
    vc_o[...] = compress(vcf_ref, 1).astype(BF16)
    ks = _rms(ks_ref[...].astype(F32), kn_ref[1:2, :])
    row = lax.broadcasted_iota(jnp.int32, ks.shape, 0)
    col = lax.broadcasted_iota(jnp.int32, ks.shape, 1)
    block_hot = jnp.where((row // SEL_LEN) % SUBLANES == col, 1.0, 0.0)
    ks_o[...] = jnp.concatenate([ks, block_hot], axis=1).astype(BF16)
    kw_o[...] = _rms(kw_ref[...].astype(F32), kn_ref[2:3, :]).astype(BF16)


def _kvprep(kc_flat, vc_flat, kv_rows, pe_flat, w1, b1, w2, b2, k_norm):
    b, g, nrow, half = kc_flat.shape
    s = kv_rows.shape[3]
    dh = kv_rows.shape[4]
    slot = lambda k: pl.BlockSpec((None, None, None, s, dh), lambda bi, gi: (k, bi, gi, 0, 0))
    bg = lambda shape: pl.BlockSpec((None, None) + shape, lambda bi, gi: (bi, gi, 0, 0))
    full = lambda a: pl.BlockSpec(a.shape, lambda bi, gi: (0,) * a.ndim)
    return pl.pallas_call(
        _kvprep_kernel,
        grid=(b, g),
        in_specs=[bg((nrow, half)), bg((nrow, half)), slot(2), slot(3),
                  full(pe_flat), full(w1), full(b1), full(w2), full(b2), full(k_norm)],
        out_specs=[bg((nrow, dh)), bg((nrow, dh)), bg((s, 2 * dh)), bg((s, dh))],
        out_shape=[jax.ShapeDtypeStruct((b, g, nrow, dh), BF16), jax.ShapeDtypeStruct((b, g, nrow, dh), BF16),
                   jax.ShapeDtypeStruct((b, g, s, 2 * dh), BF16), jax.ShapeDtypeStruct((b, g, s, dh), BF16)],
        compiler_params=_cparams("parallel", "parallel"),
        name="kvprep",
    )(kc_flat, vc_flat, kv_rows, kv_rows, pe_flat, w1, b1, w2, b2, k_norm)


def _nsa_kernel(qt_ref, kc_ref, vct_ref, ks_ref, vst_ref, kw_ref, vwt_ref, gt_ref, qn_ref, ov_ref, cb_ref, bb_ref,
                o_ref, sel_scr, *, key_tile):
    c = pl.program_id(2)
    tq = Q_BLOCK
    t0 = c * tq
    n_sel = ov_ref.shape[0]
    ncp = kc_ref.shape[0]
    top_k = min(SEL_TOPK, n_sel)
    lanes = NSA_GROUP * tq
    blocks_per_tile = key_tile // SEL_LEN

    qf = qt_ref[...].astype(F32)
    qn = qf * lax.rsqrt(jnp.mean(qf * qf, axis=0, keepdims=True) + RMS_EPS) * qn_ref[...]
    qs = (qn * (HEAD_DIM ** -0.5 * LOG2E)).astype(BF16)
    t_lane = t0 + (lax.broadcasted_iota(jnp.int32, (1, lanes), 1) & (tq - 1))

    n_tiles = (t0 + tq + key_tile - 1) // key_tile
    n_full = n_tiles - 1

    pad_rows = ks_ref.shape[1] - HEAD_DIM - 2 * SUBLANES
    assert blocks_per_tile == SUBLANES and pad_rows >= 0

    def scores(kt, bias_rows=None):
        k0 = pl.multiple_of(kt * key_tile, key_tile)
        if bias_rows is None:
            bias_rows = jnp.zeros((SUBLANES, lanes), F32)
        extra = jnp.concatenate([bias_rows, jnp.zeros((SUBLANES, lanes), F32)], axis=0).astype(BF16)
        q_aug = jnp.concatenate([qs, extra, jnp.zeros((pad_rows, lanes), BF16)], axis=0)
        return _dot(ks_ref[pl.ds(k0, key_tile), :], q_aug)

    wl = WINDOW + tq
    w0 = pl.multiple_of(jnp.maximum(t0 - WINDOW, 0), tq)
    s_c = _dot(kc_ref[...], qs)
    s_w = _dot(kw_ref[pl.ds(w0, wl), :], qs)
    s_l = scores(n_full)

    off_c = pl.multiple_of(cb_ref.shape[0] - ncp - t0 // CMP_STRIDE, SUBLANES)
    s_c = s_c + cb_ref[pl.ds(off_c, ncp), :]
    e_c = jnp.exp2(s_c - jnp.max(s_c, axis=0, keepdims=True))
    inv_c = 1.0 / jnp.sum(e_c, axis=0, keepdims=True)
    p_c = e_c * jnp.where(t_lane >= CMP_LEN - 1, inv_c, 0.0)
    o_c = _dot(vct_ref[...], p_c.astype(BF16))

    j_i = lax.broadcasted_iota(jnp.int32, (n_sel, tq), 0)
    cur = (t0 + lax.broadcasted_iota(jnp.int32, (n_sel, tq), 1)) // SEL_LEN
    ps = p_c[:, 0:tq]
    for h in range(1, NSA_GROUP):
        ps = ps + p_c[:, h * tq:(h + 1) * tq]
    ps_hi = ps.astype(BF16)
    ps_lo = (ps - ps_hi.astype(F32)).astype(BF16)
    imp = _dot(ov_ref[...], ps_hi) + _dot(ov_ref[...], ps_lo)
    forced = (j_i == 0) | (j_i == cur) | (j_i == cur - 1)
    imp = jnp.where(forced, FORCE_SCORE, imp)
    imp = jnp.where(j_i <= cur, imp, -1.0)
    slabs = [imp[v * SUBLANES:(v + 1) * SUBLANES, :] for v in range(n_sel // SUBLANES)]
    j_slab = lax.broadcasted_iota(jnp.int32, (SUBLANES, tq), 0)
    cnts = [jnp.zeros((SUBLANES, tq), jnp.int32) for _ in slabs]
    for jp in range(n_sel):
        r = imp[jp:jp + 1, :]
        for v, slab in enumerate(slabs):
            lo = v * SUBLANES
            if lo > jp:
                before = r >= slab
            elif lo + SUBLANES - 1 <= jp:
                before = r > slab
            else:
                before = (r > slab) | ((r == slab) & (j_slab + lo > jp))
            cnts[v] = cnts[v] + before.astype(jnp.int32)
    cnt = jnp.concatenate(cnts, axis=0)
    sel_bias = jnp.where((cnt < top_k) & (imp >= 0.0), 0.0, NEG_INF)
    sel_scr[...] = jnp.concatenate([sel_bias] * NSA_GROUP, axis=1)

    off_w = pl.multiple_of(WINDOW - (t0 - w0), tq)
    s_w = s_w + bb_ref[pl.ds(off_w, wl), :]
    e_w = jnp.exp2(s_w - jnp.max(s_w, axis=0, keepdims=True))
    a_w = _dot(vwt_ref[:, pl.ds(w0, wl)], e_w.astype(BF16))
    o_w = a_w[0:HEAD_DIM] * (1.0 / a_w[HEAD_DIM:HEAD_DIM + 1])

    def bias_rows(kt):
        return sel_scr[pl.ds(pl.multiple_of(kt * blocks_per_tile, SUBLANES), blocks_per_tile), :]

    def update(carry, kt, s):
        m_o, a_o = carry
        k0 = pl.multiple_of(kt * key_tile, key_tile)
        m_n = jnp.maximum(m_o, jnp.max(s, axis=0, keepdims=True))
        p = jnp.exp2(s - m_n).astype(BF16)
        a_n = jnp.exp2(m_o - m_n) * a_o + _dot(vst_ref[:, pl.ds(k0, key_tile)], p)
        return m_n, a_n

    def init():
        return jnp.full((1, lanes), NEG_INF, F32), jnp.zeros((vst_ref.shape[0], lanes), F32)

    off_l = pl.multiple_of(WINDOW - (t0 - n_full * key_tile), tq)
    last_rows = bias_rows(n_full)
    last_bias = jnp.concatenate(
        [jnp.broadcast_to(last_rows[jb:jb + 1, :], (SEL_LEN, lanes)) for jb in range(blocks_per_tile)], axis=0)
    c_first = update(init(), n_full, s_l + last_bias + bb_ref[pl.ds(off_l, key_tile), :])

    def pair_step(i, carry):
        c_a, c_b = carry
        s_a = scores(2 * i, bias_rows(2 * i))
        s_b = scores(2 * i + 1, bias_rows(2 * i + 1))
        return update(c_a, 2 * i, s_a), update(c_b, 2 * i + 1, s_b)

    c_a, c_b = lax.fori_loop(0, n_full // 2, pair_step, (c_first, init()))
    (m_a, a_a), (m_b, a_b) = c_a, lax.cond(
        n_full % 2 == 1,
        lambda c: update(c, n_full - 1, scores(n_full - 1, bias_rows(n_full - 1))),
        lambda c: c, c_b)
    m_ab = jnp.maximum(m_a, m_b)
    a_s = jnp.exp2(m_a - m_ab) * a_a + jnp.exp2(m_b - m_ab) * a_b
    o_s = a_s[0:HEAD_DIM] * (1.0 / a_s[HEAD_DIM:HEAD_DIM + 1])

    gates = jax.nn.sigmoid(gt_ref[...].astype(F32))
    for h in range(NSA_GROUP):
        hs = slice(h * tq, (h + 1) * tq)
        gb = h * N_NSA_BRANCHES
        o = (gates[gb:gb + 1, :] * o_c[:, hs] + gates[gb + 1:gb + 2, :] * o_s[:, hs]
             + gates[gb + 2:gb + 3, :] * o_w[:, hs])
        o_ref[:, h * HEAD_DIM:(h + 1) * HEAD_DIM] = o.T.astype(BF16)


def _band_bias(n_rows, lanes):
    i = jnp.arange(n_rows)[:, None]
    tl = (jnp.arange(lanes) % Q_BLOCK)[None, :]
    return jnp.where((tl < i) & (i <= tl + WINDOW), 0.0, NEG_INF).astype(F32)


def _cmp_bias(n_lead, ncp, lanes):
    i = jnp.arange(n_lead + ncp)[:, None]
    tl = (jnp.arange(lanes) % Q_BLOCK)[None, :]
    return jnp.where(CMP_STRIDE * (i - n_lead) + CMP_LEN - 1 <= tl, 0.0, NEG_INF).astype(F32)


def _nsa(qt5, k_cmp, v_cmp_t, k_sel, k_win, v_t, gates_t, q_norm_col, overlap_t, key_tile=512):
    b, g, nchunk, dh, lanes = qt5.shape
    s = k_sel.shape[2]
    ncp = k_cmp.shape[2]
    ng = gates_t.shape[2]
    n_sel = overlap_t.shape[0]
    assert key_tile <= WINDOW and key_tile % Q_BLOCK == 0 and s % key_tile == 0
    cmp_bias = _cmp_bias((s - Q_BLOCK) // CMP_STRIDE, ncp, lanes)
    band_bias = _band_bias(2 * WINDOW + Q_BLOCK, lanes)
    bg = lambda shape: pl.BlockSpec((None, None) + shape, lambda bi, gi, ci: (bi, gi, 0, 0))
    v_slot = lambda k: pl.BlockSpec((None, None, None) + v_t.shape[3:], lambda bi, gi, ci: (k, bi, gi, 0, 0))
    const = lambda a: pl.BlockSpec(a.shape, lambda bi, gi, ci: (0, 0), pipeline_mode=pl.Buffered(1))
    return pl.pallas_call(
        functools.partial(_nsa_kernel, key_tile=key_tile),
        grid=(b, g, nchunk),
        in_specs=[
            pl.BlockSpec((None, None, None, dh, lanes), lambda bi, gi, ci: (bi, gi, ci, 0, 0)),
            bg((ncp, dh)), bg((dh, ncp)), bg((s, k_sel.shape[3])), v_slot(0), bg((s, dh)), v_slot(1),
            pl.BlockSpec((None, None, ng, Q_BLOCK), lambda bi, gi, ci: (bi, gi, 0, ci)),
            pl.BlockSpec(q_norm_col.shape, lambda bi, gi, ci: (0, 0)),
            pl.BlockSpec(overlap_t.shape, lambda bi, gi, ci: (0, 0)),
            const(cmp_bias), const(band_bias),
        ],
        out_specs=pl.BlockSpec((None, Q_BLOCK, NSA_GROUP * dh), lambda bi, gi, ci: (bi, ci, gi)),
        out_shape=jax.ShapeDtypeStruct((b, s, g * NSA_GROUP * dh), BF16),
        scratch_shapes=[pltpu.VMEM((n_sel, lanes), F32)],
        compiler_params=_cparams("parallel", "parallel", "arbitrary"),
        name="nsa",
    )(qt5, k_cmp, v_cmp_t, k_sel, v_t, k_win, v_t, gates_t, q_norm_col, overlap_t, cmp_bias, band_bias)


def _merge_kernel(x_ref, yr_ref, yn_ref, sb_ref, sc_ref, sh_ref, scp_ref, shp_ref, g0_ref, g1_ref, g2_ref,
                  cw_ref, wr_ref, wn_ref, ws_ref, wo_ref, o_ref):
    ti = pl.program_id(1)
    tm = x_ref.shape[0]
    kw = cw_ref.shape[0]
    prod = sc_ref[...].astype(F32) * sh_ref[...].astype(F32)
    halo = jnp.where(ti == 0, 0.0, scp_ref[...].astype(F32) * shp_ref[...].astype(F32))
    groups_p = jnp.concatenate([halo, prod], axis=0).reshape(tm // SUBLANES + 1, SUBLANES, prod.shape[1])
    row_in = lax.broadcasted_iota(jnp.int32, (tm // SUBLANES, SUBLANES, prod.shape[1]), 1)
    conv = cw_ref[kw - 1:kw, :] * groups_p[1:]
    for shift in range(1, kw):
        rot = pltpu.roll(groups_p, shift, axis=1)
        conv = conv + cw_ref[kw - 1 - shift:kw - shift, :] * jnp.where(row_in >= shift, rot[1:], rot[:-1])
    y_sc = (sb_ref[...].astype(F32) * conv.reshape(tm, prod.shape[1])).astype(BF16)
    merged = (jax.nn.sigmoid(g0_ref[...].astype(F32)) * _dot(yr_ref[...], wr_ref[...])
              + jax.nn.sigmoid(g1_ref[...].astype(F32)) * _dot(yn_ref[...], wn_ref[...])
              + jax.nn.sigmoid(g2_ref[...].astype(F32)) * _dot(y_sc, ws_ref[...]))
    o_ref[...] = x_ref[...] + _dot(merged.astype(BF16), wo_ref[...])


def _merge(x3, proj3, y_rnn, y_nsa, sc_conv_w, w_rnn_out, w_nsa_out, w_sc_out, w_out, sc_block, gate_block, tm=512):
    b, s, d = x3.shape
    w = y_rnn.shape[2]
    rows_per = tm // SUBLANES
    tile = lambda width, col: pl.BlockSpec((None, tm, width), lambda bi, ti: (bi, ti, col))
    prev = lambda col: pl.BlockSpec((None, SUBLANES, w),
                                    lambda bi, ti: (bi, jnp.maximum(ti * rows_per - 1, 0), col))
    const = lambda a: pl.BlockSpec(a.shape, lambda bi, ti: (0, 0))
    return pl.pallas_call(
        _merge_kernel,
        grid=(b, s // tm),
        in_specs=[
            tile(d, 0), tile(w, 0), tile(w, 0),
            tile(w, sc_block), tile(w, sc_block + 1), tile(w, sc_block + 2),
            prev(sc_block + 1), prev(sc_block + 2),
            tile(d, gate_block), tile(d, gate_block + 1), tile(d, gate_block + 2),
            const(sc_conv_w), const(w_rnn_out), const(w_nsa_out), const(w_sc_out), const(w_out),
        ],
        out_specs=tile(d, 0),
        out_shape=jax.ShapeDtypeStruct((b, s, d), F32),
        compiler_params=_cparams("parallel", "parallel"),
        name="merge",
    )(x3, y_rnn, y_nsa, proj3, proj3, proj3, proj3, proj3, proj3, proj3, proj3,
      sc_conv_w, w_rnn_out, w_nsa_out, w_sc_out, w_out)


MOE_BLOCK = 1024
MOE_ROW_TILE = 304
SLOT_IND = EXPERTS_PER_GROUP
SLOT_POS = EXPERTS_PER_GROUP + 1


ROUTER_ROWS = 48


def _route_t(logits, n_experts, n_groups):
    row_i = lax.broadcasted_iota(jnp.int32, logits.shape, 0)
    row = row_i.astype(F32)
    big = float(logits.shape[0])
    is_g = (row_i >= n_experts) & (row_i < n_experts + n_groups)
    gl = jnp.where(is_g, logits, NEG_INF)
    gmax = jnp.max(gl, axis=0, keepdims=True)
    grp = jnp.min(jnp.where(is_g & (gl == gmax), row - n_experts, big), axis=0, keepdims=True)
    p_grp = 1.0 / jnp.sum(jnp.where(is_g, jnp.exp(gl - gmax), 0.0), axis=0, keepdims=True)
    in_g = (row_i < n_experts) & ((row_i // EXPERTS_PER_GROUP).astype(F32) == grp)
    el = jnp.where(in_g, logits, NEG_INF)
    ee = jnp.where(in_g, jnp.exp(el - jnp.max(el, axis=0, keepdims=True)), 0.0)
    ep = ee / jnp.sum(ee, axis=0, keepdims=True)
    top1 = jnp.max(jnp.where(in_g, ep, -1.0), axis=0, keepdims=True)
    i1 = jnp.min(jnp.where(in_g & (ep == top1), row, big), axis=0, keepdims=True)
    rest = jnp.where(in_g & (row != i1), ep, -1.0)
    top2 = jnp.max(rest, axis=0, keepdims=True)
    i2 = jnp.min(jnp.where(rest == top2, row, big), axis=0, keepdims=True)
    denom = top1 + top2
    cw = (jnp.where(row == i1, p_grp * top1 / denom, 0.0)
          + jnp.where(row == i2, p_grp * top2 / denom, 0.0))
    return cw, grp


def _router_kernel(x_ref, g_ref, wr_ref, br_ref, tri_ref, xn_ref, slot_ref, info_t_ref, *, n_experts, n_groups):
    xn = _rms(x_ref[...], g_ref[...]).astype(BF16)
    xn_ref[...] = xn
    tb = xn.shape[0]
    logits = lax.dot_general(wr_ref[...], xn, (((1,), (1,)), ((), ())), preferred_element_type=F32) + br_ref[...]
    cw, grp = _route_t(logits, n_experts, n_groups)
    g_row = lax.broadcasted_iota(jnp.int32, (SUBLANES, tb), 0)
    ind = jnp.where((g_row.astype(F32) == grp) & (g_row < n_groups), 1.0, 0.0)
    pos = _dot(ind.astype(BF16), tri_ref[...])
    info_t_ref[...] = jnp.where(g_row < n_groups, ind, pltpu.roll(pos, n_groups, axis=0))
    pad = jnp.zeros((LANES - 2 * SUBLANES, tb), F32)
    extra_row = lax.broadcasted_iota(jnp.int32, (SUBLANES, tb), 0)
    for g in range(n_groups):
        w_g = cw[g * EXPERTS_PER_GROUP:(g + 1) * EXPERTS_PER_GROUP]
        extra = jnp.where(extra_row == SLOT_IND - EXPERTS_PER_GROUP, ind[g:g + 1],
                          jnp.where(extra_row == SLOT_POS - EXPERTS_PER_GROUP, pos[g:g + 1], 0.0))
        slot_ref[:, g * LANES:(g + 1) * LANES] = jnp.concatenate([w_g, extra, pad], axis=0).T


def _router(x2, gain, w_router_t, b_router_t, n_experts, n_groups):
    n, d = x2.shape
    tb = MOE_BLOCK
    assert EXPERTS_PER_GROUP == SUBLANES and n_groups <= SUBLANES // 2
    tri = (jnp.arange(tb)[:, None] < jnp.arange(tb)[None, :]).astype(BF16)
    const = lambda a: pl.BlockSpec(a.shape, lambda i: (0, 0))
    return pl.pallas_call(
        functools.partial(_router_kernel, n_experts=n_experts, n_groups=n_groups),
        grid=(n // tb,),
        in_specs=[pl.BlockSpec((tb, d), lambda i: (i, 0)), const(gain), const(w_router_t), const(b_router_t),
                  pl.BlockSpec(tri.shape, lambda i: (0, 0), pipeline_mode=pl.Buffered(1))],
        out_specs=[pl.BlockSpec((tb, d), lambda i: (i, 0)),
                   pl.BlockSpec((tb, n_groups * LANES), lambda i: (i, 0)),
                   pl.BlockSpec((2 * n_groups, tb), lambda i: (0, i))],
        out_shape=[jax.ShapeDtypeStruct((n, d), BF16),
                   jax.ShapeDtypeStruct((n, n_groups * LANES), F32),
                   jax.ShapeDtypeStruct((2 * n_groups, n), F32)],
        compiler_params=_cparams("parallel"),
        name="router",
    )(x2, gain, w_router_t, b_router_t, tri)


def _moe_kernel(xn_ref, slot_ref, info_t_ref, wg_ref, wu_ref, wd_ref, o_ref, xc_scr, yc_scr, wc_scr, *, n_groups):
    g = pl.program_id(1)
    tb = xn_ref.shape[0]
    rt = MOE_ROW_TILE

    @pl.when(g == 0)
    def _():
        o_ref[...] = jnp.zeros_like(o_ref)

    slot = slot_ref[...]
    ind_col = slot[:, SLOT_IND:SLOT_IND + 1] > 0.5
    pos_col = slot[:, SLOT_POS:SLOT_POS + 1]
    ind_row = info_t_ref[pl.ds(g, 1), :]
    pos_row = info_t_ref[pl.ds(n_groups + g, 1), :]
    n_rows = jnp.sum(ind_row).astype(jnp.int32)
    n_tiles = (n_rows + rt - 1) // rt
    slot_hi = slot.astype(BF16)
    slot_lo = (slot - slot_hi.astype(F32)).astype(BF16)
    row_id = lax.broadcasted_iota(jnp.int32, (rt, tb), 0).astype(F32)
    col_id = lax.broadcasted_iota(jnp.int32, (tb, rt), 1).astype(F32)

    def rows(t):
        return pl.ds(pl.multiple_of(t * rt, 2 * SUBLANES), rt)

    def gather(t, _):
        base = (t * rt).astype(F32)
        pick = jnp.where((pos_row == row_id + base) & (ind_row > 0.5), 1.0, 0.0).astype(BF16)
        xc_scr[rows(t), :] = _dot(pick, xn_ref[...]).astype(BF16)
        wc_scr[rows(t), :] = _dot(pick, slot_hi) + _dot(pick, slot_lo)
        yc_scr[rows(t), :] = jnp.zeros((rt, yc_scr.shape[1]), F32)
        return 0

    lax.fori_loop(0, n_tiles, gather, 0)

    for e in range(EXPERTS_PER_GROUP):
        def expert(t, _, e=e):
            xg = xc_scr[rows(t), :]
            h = jax.nn.silu(_dot(xg, wg_ref[e])) * _dot(xg, wu_ref[e]) * wc_scr[rows(t), :][:, e:e + 1]
            yc_scr[rows(t), :] += _dot(h.astype(BF16), wd_ref[e])
            return 0

        lax.fori_loop(0, n_tiles, expert, 0)

    def scatter(t, _):
        base = (t * rt).astype(F32)
        put = jnp.where((pos_col == col_id + base) & ind_col, 1.0, 0.0).astype(BF16)
        o_ref[...] += _dot(put, yc_scr[rows(t), :].astype(BF16)).astype(BF16)
        return 0

    lax.fori_loop(0, n_tiles, scatter, 0)


def _moe(xn, slots, info_t, w_gate, w_up, w_down, layer, n_groups):
    n, d = xn.shape
    f = w_down.shape[2]
    tb = MOE_BLOCK
    epg = EXPERTS_PER_GROUP
    assert MOE_ROW_TILE % (2 * SUBLANES) == 0
    cap = -(-tb // MOE_ROW_TILE) * MOE_ROW_TILE
    return pl.pallas_call(
        functools.partial(_moe_kernel, n_groups=n_groups),
        grid=(n // tb, n_groups),
        in_specs=[
            pl.BlockSpec((tb, d), lambda i, g: (i, 0)),
            pl.BlockSpec((tb, LANES), lambda i, g: (i, g)),
            pl.BlockSpec((2 * n_groups, tb), lambda i, g: (0, i)),
            pl.BlockSpec((None, epg, d, f), lambda i, g: (layer, g, 0, 0)),
            pl.BlockSpec((None, epg, d, f), lambda i, g: (layer, g, 0, 0)),
            pl.BlockSpec((None, epg, f, d), lambda i, g: (layer, g, 0, 0)),
        ],
        out_specs=pl.BlockSpec((tb, d), lambda i, g: (i, 0)),
        out_shape=jax.ShapeDtypeStruct((n, d), BF16),
        scratch_shapes=[pltpu.VMEM((cap, d), BF16), pltpu.VMEM((cap, d), F32), pltpu.VMEM((cap, LANES), F32)],
        compiler_params=_cparams("parallel", "arbitrary"),
        name="moe",
    )(xn, slots, info_t, w_gate, w_up, w_down)


def _residual_kernel(x_ref, y_ref, o_ref):
    o_ref[...] = x_ref[...] + y_ref[...].astype(F32)


def _residual(x2, y2, tm=1024):
    n, d = x2.shape
    spec = pl.BlockSpec((tm, d), lambda i: (i, 0))
    return pl.pallas_call(
        _residual_kernel, grid=(n // tm,), in_specs=[spec, spec], out_specs=spec,
        out_shape=jax.ShapeDtypeStruct((n, d), F32), compiler_params=_cparams("parallel"), name="residual",
    )(x2, y2)


def _block_diag(w):
    nb, bs, _ = w.shape
    eye = jnp.eye(nb, dtype=w.dtype)
    return (eye[:, None, :, None] * w[:, :, None, :]).reshape(nb * bs, nb * bs)


def _overlap_t(n_sel, ncp):
    n = jnp.arange(ncp)
    j = jnp.arange(n_sel)
    c0 = n * CMP_STRIDE
    s0 = j * SEL_LEN
    ov = jnp.clip(jnp.minimum(c0[None, :] + CMP_LEN, s0[:, None] + SEL_LEN)
                  - jnp.maximum(c0[None, :], s0[:, None]), 0).astype(F32) / CMP_LEN
    return ov.astype(BF16)


def kernel(x, mix_norm, w_in, rnn_conv_w, rnn_conv_b, rg_wa, rg_ba, rg_wx, rg_bx, rg_lambda, cmp_pe, cmp_w1, cmp_b1, cmp_w2, cmp_b2, q_norm, k_norm, sc_conv_w, w_rnn_out, w_nsa_out, w_sc_out, w_out, ffn_norm, router_group_w, router_group_b, router_expert_w, router_expert_b, exp_w_gate, exp_w_up, exp_w_down):
    b, s, d = x.shape
    depth = w_in.shape[0]
    rnn_w = rnn_conv_w.shape[2]
    sc_w = sc_conv_w.shape[2]
    nsa_w = w_nsa_out.shape[1]
    dh = q_norm.shape[1]
    n_heads = nsa_w // dh
    kvh = n_heads // NSA_GROUP
    kv_w = kvh * dh
    n_groups = router_group_w.shape[2]
    n_experts = router_expert_w.shape[2]
    n = b * s
    assert rnn_w == COL_BLOCK and sc_w == COL_BLOCK and nsa_w == COL_BLOCK and dh == HEAD_DIM
    assert 4 * kv_w == COL_BLOCK and d % COL_BLOCK == 0
    assert s % COL_BLOCK == 0 and s >= WINDOW + Q_BLOCK and s % SEL_LEN == 0
    assert n_experts + n_groups <= ROUTER_ROWS and n_experts == n_groups * EXPERTS_PER_GROUP

    nrow = s // CMP_STRIDE
    n_sel = s // SEL_LEN
    n_gate = n_heads * N_NSA_BRANCHES
    nsa_cols = nsa_w + 6 * kv_w + n_gate
    c_q = 2 * COL_BLOCK
    c_sc = c_q + nsa_cols
    c_mg = c_sc + 3 * COL_BLOCK
    gate_block = 2 * COL_BLOCK // d
    sc_block = (2 * COL_BLOCK + w_in.shape[2] - c_mg) // COL_BLOCK
    overlap_t = _overlap_t(n_sel, nrow)
    chunk = CMP_STRIDE * dh

    w_main, w_nsa = _pack_w_in(w_in, c_q, c_sc, c_mg)
    w_gate_bf, w_up_bf, w_down_bf = exp_w_gate.astype(BF16), exp_w_up.astype(BF16), exp_w_down.astype(BF16)
    y_moe = None
    for l in range(depth):
        proj3, qt5, kv_rows, v_t, gates_t, x = _inproj(
            x, None if y_moe is None else y_moe.reshape(b, s, d), mix_norm[l][None, :], w_main, w_nsa, l, kvh)

        y_rnn = _rnn_branch(
            proj3, rnn_conv_w[l], rnn_conv_b[l][None, :],
            _block_diag(rg_wa[l]).astype(BF16), rg_ba[l].reshape(1, rnn_w),
            _block_diag(rg_wx[l]).astype(BF16), rg_bx[l].reshape(1, rnn_w), rg_lambda[l][None, :])

        kc_flat = kv_rows[0].reshape(b, kvh, nrow, chunk)
        vc_flat = kv_rows[1].reshape(b, kvh, nrow, chunk)
        pe_flat = cmp_pe[l].reshape(2, 2, chunk)
        k_cmp, v_cmp, k_sel, k_win = _kvprep(
            kc_flat, vc_flat, kv_rows, pe_flat, cmp_w1[l].astype(BF16), cmp_b1[l][:, None, :],
            cmp_w2[l].astype(BF16), cmp_b2[l][:, None, :], k_norm[l])
        y_nsa = _nsa(qt5, k_cmp, v_cmp.transpose(0, 1, 3, 2), k_sel, k_win, v_t, gates_t,
                     q_norm[l][:, None], overlap_t)

        x = _merge(x, proj3, y_rnn, y_nsa, sc_conv_w[l], w_rnn_out[l].astype(BF16), w_nsa_out[l].astype(BF16),
                   w_sc_out[l].astype(BF16), w_out[l].astype(BF16), sc_block, gate_block)

        n_pad = ROUTER_ROWS - n_experts - n_groups
        w_router_t = jnp.concatenate(
            [router_expert_w[l].T, router_group_w[l].T, jnp.zeros((n_pad, d), F32)], axis=0).astype(BF16)
        b_router_t = jnp.concatenate([router_expert_b[l], router_group_b[l], jnp.zeros((n_pad,), F32)])[:, None]
        xn, slots, info_t = _router(x.reshape(n, d), ffn_norm[l][None, :], w_router_t, b_router_t, n_experts, n_groups)
        y_moe = _moe(xn, slots, info_t, w_gate_bf, w_up_bf, w_down_bf, l, n_groups)
    return _residual(x.reshape(n, d), y_moe).reshape(b, s, d)
```

```python
import functools

import jax
import jax.numpy as jnp
from jax import lax
from jax.experimental import pallas as pl
from jax.experimental.pallas import tpu as pltpu

F32 = jnp.float32
BF16 = jnp.bfloat16

RMS_EPS = 1e-6
NEG_INF = -1e30
FORCE_SCORE = 1e6
RG_C = 8.0
HEAD_DIM = 64
NSA_GROUP = 4
N_NSA_BRANCHES = 3
CMP_LEN = 32
CMP_STRIDE = 16
SEL_LEN = 64
SEL_TOPK = 16
WINDOW = 512
Q_BLOCK = 512
EXPERTS_PER_GROUP = 8
TOPK_IN_GROUP = 2

LANES = 128
SUBLANES = 8
COL_BLOCK = 512
VMEM_LIMIT = 56 * 1024 * 1024
LOG2E = 1.4426950408889634


def _cparams(*sem):
    return pltpu.CompilerParams(dimension_semantics=sem, vmem_limit_bytes=VMEM_LIMIT)


def _rms(x, gain):
    return x * lax.rsqrt(jnp.mean(x * x, axis=-1, keepdims=True) + RMS_EPS) * gain


def _gelu_tanh(x):
    return 0.5 * x * (1.0 + jnp.tanh(0.7978845608028654 * (x + 0.044715 * x * x * x)))


def _dot(a, b):
    return jnp.dot(a, b, preferred_element_type=F32)


def _pack_kernel(w_ref, main_ref, nsa_ref, *, c_q, c_sc, c_mg):
    w = w_ref[...]
    n_mg = w.shape[1] - c_mg
    main_ref[:, 0:c_q] = w[:, 0:c_q].astype(BF16)
    main_ref[:, c_q:c_q + n_mg] = w[:, c_mg:].astype(BF16)
    main_ref[:, c_q + n_mg:] = w[:, c_sc:c_mg].astype(BF16)
    nsa_ref[:, 0:c_sc - c_q] = w[:, c_q:c_sc].astype(BF16)
    nsa_ref[:, c_sc - c_q:] = jnp.zeros((w.shape[0], nsa_ref.shape[1] - (c_sc - c_q)), BF16)


def _pack_w_in(w_in, c_q, c_sc, c_mg, tr=128):
    depth, d, win = w_in.shape
    w_main = win - (c_sc - c_q)
    w_nsa = -(-(c_sc - c_q) // LANES) * LANES
    return pl.pallas_call(
        functools.partial(_pack_kernel, c_q=c_q, c_sc=c_sc, c_mg=c_mg),
        grid=(depth, d // tr),
        in_specs=[pl.BlockSpec((None, tr, win), lambda l, i: (l, i, 0))],
        out_specs=[pl.BlockSpec((None, tr, w_main), lambda l, i: (l, i, 0)),
                   pl.BlockSpec((None, tr, w_nsa), lambda l, i: (l, i, 0))],
        out_shape=[jax.ShapeDtypeStruct((depth, d, w_main), BF16), jax.ShapeDtypeStruct((depth, d, w_nsa), BF16)],
        compiler_params=_cparams("parallel", "parallel"),
        name="pack_w_in",
    )(w_in)


def _inproj_kernel(*refs, has_y, kvh):
    if has_y:
        x_ref, y_ref, g_ref, w_ref, wn_ref, o_ref, qt_ref, kr_ref, vt_ref, gt_ref, xo_ref = refs
        x = x_ref[...] + y_ref[...].astype(F32)
        xo_ref[...] = x
    else:
        x_ref, g_ref, w_ref, wn_ref, o_ref, qt_ref, kr_ref, vt_ref, gt_ref = refs
        x = x_ref[...]
    xn = _rms(x, g_ref[...]).astype(BF16)
    for c in range(w_ref.shape[1] // COL_BLOCK):
        sl = slice(c * COL_BLOCK, (c + 1) * COL_BLOCK)
        o_ref[:, sl] = _dot(xn, w_ref[:, sl]).astype(BF16)

    tm = x.shape[0]
    dh = HEAD_DIM
    tq = Q_BLOCK
    nsa = _dot(xn, wn_ref[...])
    for pair in range(kvh * NSA_GROUP // 2):
        slab_t = nsa[:, pair * LANES:(pair + 1) * LANES].T
        for half in range(2):
            head = 2 * pair + half
            g, h = head // NSA_GROUP, head % NSA_GROUP
            for j in range(tm // tq):
                qt_ref[g, j, :, h * tq:(h + 1) * tq] = slab_t[half * dh:(half + 1) * dh,
                                                             j * tq:(j + 1) * tq].astype(BF16)
    kv0 = kvh * NSA_GROUP * dh
    ones_rows = jnp.where(lax.broadcasted_iota(jnp.int32, (2 * SUBLANES, tm), 0) == 0, 1.0, 0.0).astype(BF16)
    for slab, (row_slot, col_slot) in enumerate(((0, None), (1, None), (2, None), (None, 0), (3, None), (None, 1))):
        v = nsa[:, kv0 + slab * kvh * dh:kv0 + (slab + 1) * kvh * dh]
        if row_slot is not None:
            for g in range(kvh):
                kr_ref[row_slot, g] = v[:, g * dh:(g + 1) * dh].astype(BF16)
        else:
            v_t = v.T
            for g in range(kvh):
                vt_ref[col_slot, g, 0:dh, :] = v_t[g * dh:(g + 1) * dh].astype(BF16)
                vt_ref[col_slot, g, dh:dh + 2 * SUBLANES, :] = ones_rows
    g0 = kv0 + 6 * kvh * dh
    gates_t = nsa[:, g0:g0 + LANES].T
    per_group = NSA_GROUP * N_NSA_BRANCHES
    for g in range(kvh):
        gt_ref[g] = gates_t[g * per_group:(g + 1) * per_group].astype(BF16)


def _inproj(x3, y3, gain, w_main, w_nsa, layer, kvh, tm=512):
    b, s, d = x3.shape
    wp = w_main.shape[2]
    dh = HEAD_DIM
    row = pl.BlockSpec((None, tm, d), lambda bi, ti: (bi, ti, 0))
    has_y = y3 is not None
    lanes = NSA_GROUP * Q_BLOCK
    per_group = NSA_GROUP * N_NSA_BRANCHES
    out_specs = [
        pl.BlockSpec((None, tm, wp), lambda bi, ti: (bi, ti, 0)),
        pl.BlockSpec((None, kvh, tm // Q_BLOCK, dh, lanes), lambda bi, ti: (bi, 0, ti, 0, 0)),
        pl.BlockSpec((4, None, kvh, tm, dh), lambda bi, ti: (0, bi, 0, ti, 0)),
        pl.BlockSpec((2, None, kvh, dh + 2 * SUBLANES, tm), lambda bi, ti: (0, bi, 0, 0, ti)),
        pl.BlockSpec((None, kvh, per_group, tm), lambda bi, ti: (bi, 0, 0, ti)),
    ] + ([row] if has_y else [])
    out_shape = [
        jax.ShapeDtypeStruct((b, s, wp), BF16),
        jax.ShapeDtypeStruct((b, kvh, s // Q_BLOCK, dh, lanes), BF16),
        jax.ShapeDtypeStruct((4, b, kvh, s, dh), BF16),
        jax.ShapeDtypeStruct((2, b, kvh, dh + 2 * SUBLANES, s), BF16),
        jax.ShapeDtypeStruct((b, kvh, per_group, s), BF16),
    ] + ([jax.ShapeDtypeStruct((b, s, d), F32)] if has_y else [])
    const = lambda a: pl.BlockSpec((None,) + a.shape[1:], lambda bi, ti: (layer, 0, 0), pipeline_mode=pl.Buffered(1))
    res = pl.pallas_call(
        functools.partial(_inproj_kernel, has_y=has_y, kvh=kvh),
        grid=(b, s // tm),
        in_specs=[row] + ([row] if has_y else []) + [pl.BlockSpec((1, d), lambda bi, ti: (0, 0)),
                                                    const(w_main), const(w_nsa)],
        out_specs=out_specs,
        out_shape=out_shape,
        compiler_params=_cparams("parallel", "parallel"),
        name="inproj",
    )(*([x3, y3] if has_y else [x3]), gain, w_main, w_nsa)
    return tuple(res[:5]) + ((res[5],) if has_y else (x3,))


def _rnn_kernel(x_ref, xp_ref, y_ref, cw_ref, cb_ref, wa_ref, ba_ref, wx_ref, bx_ref, lam_ref,
                o_ref, h_scr):
    ti = pl.program_id(1)
    ts = x_ref.shape[0]
    kw = cw_ref.shape[0]

    @pl.when(ti == 0)
    def _():
        h_scr[...] = jnp.zeros_like(h_scr)

    x = x_ref[...].astype(F32)
    halo = jnp.where(ti == 0, 0.0, xp_ref[...].astype(F32))
    groups_x = jnp.concatenate([halo, x], axis=0).reshape(ts // SUBLANES + 1, SUBLANES, x.shape[1])
    row_in = lax.broadcasted_iota(jnp.int32, (ts // SUBLANES, SUBLANES, x.shape[1]), 1)
    u = cb_ref[...] + cw_ref[kw - 1:kw, :] * groups_x[1:]
    for shift in range(1, kw):
        rot = pltpu.roll(groups_x, shift, axis=1)
        u = u + cw_ref[kw - 1 - shift:kw - shift, :] * jnp.where(row_in >= shift, rot[1:], rot[:-1])
    u = u.reshape(ts, x.shape[1])
    ub = u.astype(BF16)
    r = jax.nn.sigmoid(_dot(ub, wa_ref[...]) + ba_ref[...])
    gi = jax.nn.sigmoid(_dot(ub, wx_ref[...]) + bx_ref[...])
    lam = lam_ref[...]
    softplus_neg = jnp.maximum(-lam, 0.0) + jnp.log(1.0 + jnp.exp(-jnp.abs(lam)))
    log_a = -RG_C * r * softplus_neg
    a = jnp.exp(log_a)
    b = jnp.sqrt(1.0 - jnp.exp(2.0 * log_a)) * (gi * u)

    a = a.reshape(ts // SUBLANES, SUBLANES, a.shape[1])
    b = b.reshape(a.shape)
    row_in_group = lax.broadcasted_iota(jnp.int32, a.shape, 1)
    d = 1
    while d < SUBLANES:
        keep = row_in_group >= d
        a_sh = pltpu.roll(a, d, axis=1)
        b_sh = pltpu.roll(b, d, axis=1)
        b = jnp.where(keep, a * b_sh + b, b)
        a = jnp.where(keep, a * a_sh, a)
        d *= 2
    h_prev = h_scr[...]
    groups = []
    for r in range(ts // SUBLANES):
        h_r = a[r] * h_prev + b[r]
        groups.append(h_r)
        h_prev = h_r[SUBLANES - 1:SUBLANES, :]
    h = jnp.concatenate(groups, axis=0)
    h_scr[...] = h_prev
    o_ref[...] = (_gelu_tanh(y_ref[...].astype(F32)) * h).astype(BF16)


def _rnn_branch(proj3, conv_w, conv_b, wa_bd, ba, wx_bd, bx, lam, ts=512):
    b, s, _ = proj3.shape
    w = conv_w.shape[1]
    nb = w // COL_BLOCK
    rows_per = ts // SUBLANES
    const = lambda shape: pl.BlockSpec(shape, lambda bi, ti: (0, 0))
    return pl.pallas_call(
        _rnn_kernel,
        grid=(b, s // ts),
        in_specs=[
            pl.BlockSpec((None, ts, w), lambda bi, ti: (bi, ti, 0)),
            pl.BlockSpec((None, SUBLANES, w), lambda bi, ti: (bi, jnp.maximum(ti * rows_per - 1, 0), 0)),
            pl.BlockSpec((None, ts, w), lambda bi, ti: (bi, ti, nb)),
            const(conv_w.shape), const((1, w)), const((w, w)), const((1, w)), const((w, w)), const((1, w)),
            const((1, w)),
        ],
        out_specs=pl.BlockSpec((None, ts, w), lambda bi, ti: (bi, ti, 0)),
        out_shape=jax.ShapeDtypeStruct((b, s, w), BF16),
        scratch_shapes=[pltpu.VMEM((1, w), F32)],
        compiler_params=_cparams("parallel", "arbitrary"),
        name="rnn_branch",
    )(proj3, proj3, proj3, conv_w, conv_b, wa_bd, ba, wx_bd, bx, lam)


def _kvprep_kernel(kcf_ref, vcf_ref, ks_ref, kw_ref, pe_ref, w1_ref, b1_ref, w2_ref, b2_ref, kn_ref,
                   kc_o, vc_o, ks_o, kw_o):
    half = kcf_ref.shape[1]
    nrow = kcf_ref.shape[0]

    def compress(flat_ref, i):
        xf = flat_ref[...].astype(F32)
        xa = (xf + pe_ref[i, 0:1, :]).astype(BF16)
        xb = (xf + pe_ref[i, 1:2, :]).astype(BF16)
        first = _dot(xa, w1_ref[i, 0:half, :])
        second = _dot(xb, w1_ref[i, half:2 * half, :])
        hid = first + pltpu.roll(second, nrow - 1, axis=0) + b1_ref[i]
        hid = _gelu_tanh(hid).astype(BF16)
        return _dot(hid, w2_ref[i]) + b2_ref[i]

    kc_o[...] = _rms(compress(kcf_ref, 0), kn_ref[0:1, :]).astype(BF16)
    vc_o[...] = compress(vcf_ref, 1).astype(BF16)
    ks = _rms(ks_ref[...].astype(F32), kn_ref[1:2, :])
    row = lax.broadcasted_iota(jnp.int32, ks.shape, 0)
    col = lax.broadcasted_iota(jnp.int32, ks.shape, 1)
    block_hot = jnp.where((row // SEL_LEN) % SUBLANES == col, 1.0, 0.0)
    ks_o[...] = jnp.concatenate([ks, block_hot], axis=1).astype(BF16)
    kw_o[...] = _rms(kw_ref[...].astype(F32), kn_ref[2:3, :]).astype(BF16)


def _kvprep(kc_flat, vc_flat, kv_rows, pe_flat, w1, b1, w2, b2, k_norm):
    b, g, nrow, half = kc_flat.shape
    s = kv_rows.shape[3]
    dh = kv_rows.shape[4]
    slot = lambda k: pl.BlockSpec((None, None, None, s, dh), lambda bi, gi: (k, bi, gi, 0, 0))
    bg = lambda shape: pl.BlockSpec((None, None) + shape, lambda bi, gi: (bi, gi, 0, 0))
    full = lambda a: pl.BlockSpec(a.shape, lambda bi, gi: (0,) * a.ndim)
    return pl.pallas_call(
        _kvprep_kernel,
        grid=(b, g),
        in_specs=[bg((nrow, half)), bg((nrow, half)), slot(2), slot(3),
                  full(pe_flat), full(w1), full(b1), full(w2), full(b2), full(k_norm)],
        out_specs=[bg((nrow, dh)), bg((nrow, dh)), bg((s, 2 * dh)), bg((s, dh))],
        out_shape=[jax.ShapeDtypeStruct((b, g, nrow, dh), BF16), jax.ShapeDtypeStruct((b, g, nrow, dh), BF16),
                   jax.ShapeDtypeStruct((b, g, s, 2 * dh), BF16), jax.ShapeDtypeStruct((b, g, s, dh), BF16)],
        compiler_params=_cparams("parallel", "parallel"),
        name="kvprep",
    )(kc_flat, vc_flat, kv_rows, kv_rows, pe_flat, w1, b1, w2, b2, k_norm)


def _nsa_kernel(qt_ref, kc_ref, vct_ref, ks_ref, vst_ref, kw_ref, vwt_ref, gt_ref, qn_ref, ov_ref, cb_ref, bb_ref,
                o_ref, sel_scr, *, key_tile):
    c = pl.program_id(2)
    tq = Q_BLOCK
    t0 = c * tq
    n_sel = ov_ref.shape[0]
    ncp = kc_ref.shape[0]
    top_k = min(SEL_TOPK, n_sel)
    lanes = NSA_GROUP * tq
    blocks_per_tile = key_tile // SEL_LEN

    qf = qt_ref[...].astype(F32)
    qn = qf * lax.rsqrt(jnp.mean(qf * qf, axis=0, keepdims=True) + RMS_EPS) * qn_ref[...]
    qs = (qn * (HEAD_DIM ** -0.5 * LOG2E)).astype(BF16)
    t_lane = t0 + (lax.broadcasted_iota(jnp.int32, (1, lanes), 1) & (tq - 1))

    n_tiles = (t0 + tq + key_tile - 1) // key_tile
    n_full = n_tiles - 1

    pad_rows = ks_ref.shape[1] - HEAD_DIM - 2 * SUBLANES
    assert blocks_per_tile == SUBLANES and pad_rows >= 0

    def scores(kt, bias_rows=None):
        k0 = pl.multiple_of(kt * key_tile, key_tile)
        if bias_rows is None:
            bias_rows = jnp.zeros((SUBLANES, lanes), F32)
        extra = jnp.concatenate([bias_rows, jnp.zeros((SUBLANES, lanes), F32)], axis=0).astype(BF16)
        q_aug = jnp.concatenate([qs, extra, jnp.zeros((pad_rows, lanes), BF16)], axis=0)
        return _dot(ks_ref[pl.ds(k0, key_tile), :], q_aug)

    wl = WINDOW + tq
    w0 = pl.multiple_of(jnp.maximum(t0 - WINDOW, 0), tq)
    s_c = _dot(kc_ref[...], qs)
    s_w = _dot(kw_ref[pl.ds(w0, wl), :], qs)
    s_l = scores(n_full)

    off_c = pl.multiple_of(cb_ref.shape[0] - ncp - t0 // CMP_STRIDE, SUBLANES)
    s_c = s_c + cb_ref[pl.ds(off_c, ncp), :]
    e_c = jnp.exp2(s_c - jnp.max(s_c, axis=0, keepdims=True))
    inv_c = 1.0 / jnp.sum(e_c, axis=0, keepdims=True)
    p_c = e_c * jnp.where(t_lane >= CMP_LEN - 1, inv_c, 0.0)
    o_c = _dot(vct_ref[...], p_c.astype(BF16))

    j_i = lax.broadcasted_iota(jnp.int32, (n_sel, tq), 0)
    cur = (t0 + lax.broadcasted_iota(jnp.int32, (n_sel, tq), 1)) // SEL_LEN
    ps = p_c[:, 0:tq]
    for h in range(1, NSA_GROUP):
        ps = ps + p_c[:, h * tq:(h + 1) * tq]
    ps_hi = ps.astype(BF16)
    ps_lo = (ps - ps_hi.astype(F32)).astype(BF16)
    imp = _dot(ov_ref[...], ps_hi) + _dot(ov_ref[...], ps_lo)
    forced = (j_i == 0) | (j_i == cur) | (j_i == cur - 1)
    imp = jnp.where(forced, FORCE_SCORE, imp)
    imp = jnp.where(j_i <= cur, imp, -1.0)
    slabs = [imp[v * SUBLANES:(v + 1) * SUBLANES, :] for v in range(n_sel // SUBLANES)]
    j_slab = lax.broadcasted_iota(jnp.int32, (SUBLANES, tq), 0)
    cnts = [jnp.zeros((SUBLANES, tq), jnp.int32) for _ in slabs]
    for jp in range(n_sel):
        r = imp[jp:jp + 1, :]
        for v, slab in enumerate(slabs):
            lo = v * SUBLANES
            if lo > jp:
                before = r >= slab
            elif lo + SUBLANES - 1 <= jp:
                before = r > slab
            else:
                before = (r > slab) | ((r == slab) & (j_slab + lo > jp))
            cnts[v] = cnts[v] + before.astype(jnp.int32)
    cnt = jnp.concatenate(cnts, axis=0)
    sel_bias = jnp.where((cnt < top_k) & (imp >= 0.0), 0.0, NEG_INF)
    sel_scr[...] = jnp.concatenate([sel_bias] * NSA_GROUP, axis=1)

    off_w = pl.multiple_of(WINDOW - (t0 - w0), tq)
    s_w = s_w + bb_ref[pl.ds(off_w, wl), :]
    e_w = jnp.exp2(s_w - jnp.max(s_w, axis=0, keepdims=True))
    a_w = _dot(vwt_ref[:, pl.ds(w0, wl)], e_w.astype(BF16))
    o_w = a_w[0:HEAD_DIM] * (1.0 / a_w[HEAD_DIM:HEAD_DIM + 1])

    def bias_rows(kt):
        return sel_scr[pl.ds(pl.multiple_of(kt * blocks_per_tile, SUBLANES), blocks_per_tile), :]

    def update(carry, kt, s):
        m_o, a_o = carry
        k0 = pl.multiple_of(kt * key_tile, key_tile)
        m_n = jnp.maximum(m_o, jnp.max(s, axis=0, keepdims=True))
        p = jnp.exp2(s - m_n).astype(BF16)
        a_n = jnp.exp2(m_o - m_n) * a_o + _dot(vst_ref[:, pl.ds(k0, key_tile)], p)
        return m_n, a_n

    def init():
        return jnp.full((1, lanes), NEG_INF, F32), jnp.zeros((vst_ref.shape[0], lanes), F32)

    off_l = pl.multiple_of(WINDOW - (t0 - n_full * key_tile), tq)
    last_rows = bias_rows(n_full)
    last_bias = jnp.concatenate(
        [jnp.broadcast_to(last_rows[jb:jb + 1, :], (SEL_LEN, lanes)) for jb in range(blocks_per_tile)], axis=0)
    c_first = update(init(), n_full, s_l + last_bias + bb_ref[pl.ds(off_l, key_tile), :])

    def pair_step(i, carry):
        c_a, c_b = carry
        s_a = scores(2 * i, bias_rows(2 * i))
        s_b = scores(2 * i + 1, bias_rows(2 * i + 1))
        return update(c_a, 2 * i, s_a), update(c_b, 2 * i + 1, s_b)

    c_a, c_b = lax.fori_loop(0, n_full // 2, pair_step, (c_first, init()))
    (m_a, a_a), (m_b, a_b) = c_a, lax.cond(
        n_full % 2 == 1,
        lambda c: update(c, n_full - 1, scores(n_full - 1, bias_rows(n_full - 1))),
        lambda c: c, c_b)
    m_ab = jnp.maximum(m_a, m_b)
    a_s = jnp.exp2(m_a - m_ab) * a_a + jnp.exp2(m_b - m_ab) * a_b
    o_s = a_s[0:HEAD_DIM] * (1.0 / a_s[HEAD_DIM:HEAD_DIM + 1])

    gates = jax.nn.sigmoid(gt_ref[...].astype(F32))
    for h in range(NSA_GROUP):
        hs = slice(h * tq, (h + 1) * tq)
        gb = h * N_NSA_BRANCHES
        o = (gates[gb:gb + 1, :] * o_c[:, hs] + gates[gb + 1:gb + 2, :] * o_s[:, hs]
             + gates[gb + 2:gb + 3, :] * o_w[:, hs])
        o_ref[:, h * HEAD_DIM:(h + 1) * HEAD_DIM] = o.T.astype(BF16)


def _band_bias(n_rows, lanes):
    i = jnp.arange(n_rows)[:, None]
    tl = (jnp.arange(lanes) % Q_BLOCK)[None, :]
    return jnp.where((tl < i) & (i <= tl + WINDOW), 0.0, NEG_INF).astype(F32)


def _cmp_bias(n_lead, ncp, lanes):
    i = jnp.arange(n_lead + ncp)[:, None]
    tl = (jnp.arange(lanes) % Q_BLOCK)[None, :]
    return jnp.where(CMP_STRIDE * (i - n_lead) + CMP_LEN - 1 <= tl, 0.0, NEG_INF).astype(F32)


def _nsa(qt5, k_cmp, v_cmp_t, k_sel, k_win, v_t, gates_t, q_norm_col, overlap_t, key_tile=512):
    b, g, nchunk, dh, lanes = qt5.shape
    s = k_sel.shape[2]
    ncp = k_cmp.shape[2]
    ng = gates_t.shape[2]
    n_sel = overlap_t.shape[0]
    assert key_tile <= WINDOW and key_tile % Q_BLOCK == 0 and s % key_tile == 0
    cmp_bias = _cmp_bias((s - Q_BLOCK) // CMP_STRIDE, ncp, lanes)
    band_bias = _band_bias(2 * WINDOW + Q_BLOCK, lanes)
    bg = lambda shape: pl.BlockSpec((None, None) + shape, lambda bi, gi, ci: (bi, gi, 0, 0))
    v_slot = lambda k: pl.BlockSpec((None, None, None) + v_t.shape[3:], lambda bi, gi, ci: (k, bi, gi, 0, 0))
    const = lambda a: pl.BlockSpec(a.shape, lambda bi, gi, ci: (0, 0), pipeline_mode=pl.Buffered(1))
    return pl.pallas_call(
        functools.partial(_nsa_kernel, key_tile=key_tile),
        grid=(b, g, nchunk),
        in_specs=[
            pl.BlockSpec((None, None, None, dh, lanes), lambda bi, gi, ci: (bi, gi, ci, 0, 0)),
            bg((ncp, dh)), bg((dh, ncp)), bg((s, k_sel.shape[3])), v_slot(0), bg((s, dh)), v_slot(1),
            pl.BlockSpec((None, None, ng, Q_BLOCK), lambda bi, gi, ci: (bi, gi, 0, ci)),
            pl.BlockSpec(q_norm_col.shape, lambda bi, gi, ci: (0, 0)),
            pl.BlockSpec(overlap_t.shape, lambda bi, gi, ci: (0, 0)),
            const(cmp_bias), const(band_bias),
        ],
        out_specs=pl.BlockSpec((None, Q_BLOCK, NSA_GROUP * dh), lambda bi, gi, ci: (bi, ci, gi)),
        out_shape=jax.ShapeDtypeStruct((b, s, g * NSA_GROUP * dh), BF16),
        scratch_shapes=[pltpu.VMEM((n_sel, lanes), F32)],
        compiler_params=_cparams("parallel", "parallel", "arbitrary"),
        name="nsa",
    )(qt5, k_cmp, v_cmp_t, k_sel, v_t, k_win, v_t, gates_t, q_norm_col, overlap_t, cmp_bias, band_bias)


def _merge_kernel(x_ref, yr_ref, yn_ref, sb_ref, sc_ref, sh_ref, scp_ref, shp_ref, g0_ref, g1_ref, g2_ref,
                  cw_ref, wr_ref, wn_ref, ws_ref, wo_ref, o_ref):
    ti = pl.program_id(1)
    tm = x_ref.shape[0]
    kw = cw_ref.shape[0]
    prod = sc_ref[...].astype(F32) * sh_ref[...].astype(F32)
    halo = jnp.where(ti == 0, 0.0, scp_ref[...].astype(F32) * shp_ref[...].astype(F32))
    groups_p = jnp.concatenate([halo, prod], axis=0).reshape(tm // SUBLANES + 1, SUBLANES, prod.shape[1])
    row_in = lax.broadcasted_iota(jnp.int32, (tm // SUBLANES, SUBLANES, prod.shape[1]), 1)
    conv = cw_ref[kw - 1:kw, :] * groups_p[1:]
    for shift in range(1, kw):
        rot = pltpu.roll(groups_p, shift, axis=1)
        conv = conv + cw_ref[kw - 1 - shift:kw - shift, :] * jnp.where(row_in >= shift, rot[1:], rot[:-1])
    y_sc = (sb_ref[...].astype(F32) * conv.reshape(tm, prod.shape[1])).astype(BF16)
    merged = (jax.nn.sigmoid(g0_ref[...].astype(F32)) * _dot(yr_ref[...], wr_ref[...])
              + jax.nn.sigmoid(g1_ref[...].astype(F32)) * _dot(yn_ref[...], wn_ref[...])
              + jax.nn.sigmoid(g2_ref[...].astype(F32)) * _dot(y_sc, ws_ref[...]))
    o_ref[...] = x_ref[...] + _dot(merged.astype(BF16), wo_ref[...])


def _merge(x3, proj3, y_rnn, y_nsa, sc_conv_w, w_rnn_out, w_nsa_out, w_sc_out, w_out, sc_block, gate_block, tm=512):
    b, s, d = x3.shape
    w = y_rnn.shape[2]
    rows_per = tm // SUBLANES
    tile = lambda width, col: pl.BlockSpec((None, tm, width), lambda bi, ti: (bi, ti, col))
    prev = lambda col: pl.BlockSpec((None, SUBLANES, w),
                                    lambda bi, ti: (bi, jnp.maximum(ti * rows_per - 1, 0), col))
    const = lambda a: pl.BlockSpec(a.shape, lambda bi, ti: (0, 0))
    return pl.pallas_call(
        _merge_kernel,
        grid=(b, s // tm),
        in_specs=[
            tile(d, 0), tile(w, 0), tile(w, 0),
            tile(w, sc_block), tile(w, sc_block + 1), tile(w, sc_block + 2),
            prev(sc_block + 1), prev(sc_block + 2),
            tile(d, gate_block), tile(d, gate_block + 1), tile(d, gate_block + 2),
            const(sc_conv_w), const(w_rnn_out), const(w_nsa_out), const(w_sc_out), const(w_out),
        ],
        out_specs=tile(d, 0),
        out_shape=jax.ShapeDtypeStruct((b, s, d), F32),
        compiler_params=_cparams("parallel", "parallel"),
        name="merge",
    )(x3, y_rnn, y_nsa, proj3, proj3, proj3, proj3, proj3, proj3, proj3, proj3,
      sc_conv_w, w_rnn_out, w_nsa_out, w_sc_out, w_out)


MOE_BLOCK = 1024
MOE_ROW_TILE = 304
SLOT_IND = EXPERTS_PER_GROUP
SLOT_POS = EXPERTS_PER_GROUP + 1


ROUTER_ROWS = 48


def _route_t(logits, n_experts, n_groups):
    row_i = lax.broadcasted_iota(jnp.int32, logits.shape, 0)
    row = row_i.astype(F32)
    big = float(logits.shape[0])
    is_g = (row_i >= n_experts) & (row_i < n_experts + n_groups)
    gl = jnp.where(is_g, logits, NEG_INF)
    gmax = jnp.max(gl, axis=0, keepdims=True)
    grp = jnp.min(jnp.where(is_g & (gl == gmax), row - n_experts, big), axis=0, keepdims=True)
    p_grp = 1.0 / jnp.sum(jnp.where(is_g, jnp.exp(gl - gmax), 0.0), axis=0, keepdims=True)
    in_g = (row_i < n_experts) & ((row_i // EXPERTS_PER_GROUP).astype(F32) == grp)
    el = jnp.where(in_g, logits, NEG_INF)
    ee = jnp.where(in_g, jnp.exp(el - jnp.max(el, axis=0, keepdims=True)), 0.0)
    ep = ee / jnp.sum(ee, axis=0, keepdims=True)
    top1 = jnp.max(jnp.where(in_g, ep, -1.0), axis=0, keepdims=True)
    i1 = jnp.min(jnp.where(in_g & (ep == top1), row, big), axis=0, keepdims=True)
    rest = jnp.where(in_g & (row != i1), ep, -1.0)
    top2 = jnp.max(rest, axis=0, keepdims=True)
    i2 = jnp.min(jnp.where(rest == top2, row, big), axis=0, keepdims=True)
    denom = top1 + top2
    cw = (jnp.where(row == i1, p_grp * top1 / denom, 0.0)
          + jnp.where(row == i2, p_grp * top2 / denom, 0.0))
    return cw, grp


def _router_kernel(x_ref, g_ref, wr_ref, br_ref, tri_ref, xn_ref, slot_ref, info_t_ref, *, n_experts, n_groups):
    xn = _rms(x_ref[...], g_ref[...]).astype(BF16)
    xn_ref[...] = xn
    tb = xn.shape[0]
    logits = lax.dot_general(wr_ref[...], xn, (((1,), (1,)), ((), ())), preferred_element_type=F32) + br_ref[...]
    cw, grp = _route_t(logits, n_experts, n_groups)
    g_row = lax.broadcasted_iota(jnp.int32, (SUBLANES, tb), 0)
    ind = jnp.where((g_row.astype(F32) == grp) & (g_row < n_groups), 1.0, 0.0)
    pos = _dot(ind.astype(BF16), tri_ref[...])
    info_t_ref[...] = jnp.where(g_row < n_groups, ind, pltpu.roll(pos, n_groups, axis=0))
    pad = jnp.zeros((LANES - 2 * SUBLANES, tb), F32)
    extra_row = lax.broadcasted_iota(jnp.int32, (SUBLANES, tb), 0)
    for g in range(n_groups):
        w_g = cw[g * EXPERTS_PER_GROUP:(g + 1) * EXPERTS_PER_GROUP]
        extra = jnp.where(extra_row == SLOT_IND - EXPERTS_PER_GROUP, ind[g:g + 1],
                          jnp.where(extra_row == SLOT_POS - EXPERTS_PER_GROUP, pos[g:g + 1], 0.0))
        slot_ref[:, g * LANES:(g + 1) * LANES] = jnp.concatenate([w_g, extra, pad], axis=0).T


def _router(x2, gain, w_router_t, b_router_t, n_experts, n_groups):
    n, d = x2.shape
    tb = MOE_BLOCK
    assert EXPERTS_PER_GROUP == SUBLANES and n_groups <= SUBLANES // 2
    tri = (jnp.arange(tb)[:, None] < jnp.arange(tb)[None, :]).astype(BF16)
    const = lambda a: pl.BlockSpec(a.shape, lambda i: (0, 0))
    return pl.pallas_call(
        functools.partial(_router_kernel, n_experts=n_experts, n_groups=n_groups),
        grid=(n // tb,),
        in_specs=[pl.BlockSpec((tb, d), lambda i: (i, 0)), const(gain), const(w_router_t), const(b_router_t),
                  pl.BlockSpec(tri.shape, lambda i: (0, 0), pipeline_mode=pl.Buffered(1))],
        out_specs=[pl.BlockSpec((tb, d), lambda i: (i, 0)),
                   pl.BlockSpec((tb, n_groups * LANES), lambda i: (i, 0)),
                   pl.BlockSpec((2 * n_groups, tb), lambda i: (0, i))],
        out_shape=[jax.ShapeDtypeStruct((n, d), BF16),
                   jax.ShapeDtypeStruct((n, n_groups * LANES), F32),
                   jax.ShapeDtypeStruct((2 * n_groups, n), F32)],
        compiler_params=_cparams("parallel"),
        name="router",
    )(x2, gain, w_router_t, b_router_t, tri)


def _moe_kernel(xn_ref, slot_ref, info_t_ref, wg_ref, wu_ref, wd_ref, o_ref, *, n_groups):
    g = pl.program_id(1)
    tb = xn_ref.shape[0]
    rt = MOE_ROW_TILE

    @pl.when(g == 0)
    def _():
        o_ref[...] = jnp.zeros_like(o_ref)

    slot = slot_ref[...]
    ind_col = slot[:, SLOT_IND:SLOT_IND + 1] > 0.5
    pos_col = slot[:, SLOT_POS:SLOT_POS + 1]
    ind_row = info_t_ref[pl.ds(g, 1), :]
    pos_row = info_t_ref[pl.ds(n_groups + g, 1), :]
    n_rows = jnp.sum(ind_row).astype(jnp.int32)
    n_tiles = (n_rows + rt - 1) // rt
    slot_hi = slot.astype(BF16)
    slot_lo = (slot - slot_hi.astype(F32)).astype(BF16)
    row_id = lax.broadcasted_iota(jnp.int32, (rt, tb), 0).astype(F32)
    col_id = lax.broadcasted_iota(jnp.int32, (tb, rt), 1).astype(F32)

    def tile(t, _):
        base = (t * rt).astype(F32)
        pick = jnp.where((pos_row == row_id + base) & (ind_row > 0.5), 1.0, 0.0).astype(BF16)
        xg = _dot(pick, xn_ref[...]).astype(BF16)
        wc = _dot(pick, slot_hi) + _dot(pick, slot_lo)
        y = None
        for e in range(EXPERTS_PER_GROUP):
            h = jax.nn.silu(_dot(xg, wg_ref[e])) * _dot(xg, wu_ref[e]) * wc[:, e:e + 1]
            y_e = _dot(h.astype(BF16), wd_ref[e])
            y = y_e if y is None else y + y_e
        put = jnp.where((pos_col == col_id + base) & ind_col, 1.0, 0.0).astype(BF16)
        o_ref[...] += _dot(put, y.astype(BF16)).astype(BF16)
        return 0

    lax.fori_loop(0, n_tiles, tile, 0)


def _moe(xn, slots, info_t, w_gate, w_up, w_down, layer, n_groups):
    n, d = xn.shape
    f = w_down.shape[2]
    tb = MOE_BLOCK
    epg = EXPERTS_PER_GROUP
    assert MOE_ROW_TILE % (2 * SUBLANES) == 0
    return pl.pallas_call(
        functools.partial(_moe_kernel, n_groups=n_groups),
        grid=(n // tb, n_groups),
        in_specs=[
            pl.BlockSpec((tb, d), lambda i, g: (i, 0)),
            pl.BlockSpec((tb, LANES), lambda i, g: (i, g)),
            pl.BlockSpec((2 * n_groups, tb), lambda i, g: (0, i)),
            pl.BlockSpec((None, epg, d, f), lambda i, g: (layer, g, 0, 0)),
            pl.BlockSpec((None, epg, d, f), lambda i, g: (layer, g, 0, 0)),
            pl.BlockSpec((None, epg, f, d), lambda i, g: (layer, g, 0, 0)),
        ],
        out_specs=pl.BlockSpec((tb, d), lambda i, g: (i, 0)),
        out_shape=jax.ShapeDtypeStruct((n, d), BF16),
        compiler_params=_cparams("parallel", "arbitrary"),
        name="moe",
    )(xn, slots, info_t, w_gate, w_up, w_down)


def _residual_kernel(x_ref, y_ref, o_ref):
    o_ref[...] = x_ref[...] + y_ref[...].astype(F32)


def _residual(x2, y2, tm=1024):
    n, d = x2.shape
    spec = pl.BlockSpec((tm, d), lambda i: (i, 0))
    return pl.pallas_call(
        _residual_kernel, grid=(n // tm,), in_specs=[spec, spec], out_specs=spec,
        out_shape=jax.ShapeDtypeStruct((n, d), F32), compiler_params=_cparams("parallel"), name="residual",
    )(x2, y2)


def _block_diag(w):
    nb, bs, _ = w.shape
    eye = jnp.eye(nb, dtype=w.dtype)
    return (eye[:, None, :, None] * w[:, :, None, :]).reshape(nb * bs, nb * bs)


def _overlap_t(n_sel, ncp):
    n = jnp.arange(ncp)
    j = jnp.arange(n_sel)
    c0 = n * CMP_STRIDE
    s0 = j * SEL_LEN
    ov = jnp.clip(jnp.minimum(c0[None, :] + CMP_LEN, s0[:, None] + SEL_LEN)
                  - jnp.maximum(c0[None, :], s0[:, None]), 0).astype(F32) / CMP_LEN
    return ov.astype(BF16)


def kernel(x, mix_norm, w_in, rnn_conv_w, rnn_conv_b, rg_wa, rg_ba, rg_wx, rg_bx, rg_lambda, cmp_pe, cmp_w1, cmp_b1, cmp_w2, cmp_b2, q_norm, k_norm, sc_conv_w, w_rnn_out, w_nsa_out, w_sc_out, w_out, ffn_norm, router_group_w, router_group_b, router_expert_w, router_expert_b, exp_w_gate, exp_w_up, exp_w_down):
    b, s, d = x.shape
    depth = w_in.shape[0]
    rnn_w = rnn_conv_w.shape[2]
    sc_w = sc_conv_w.shape[2]
    nsa_w = w_nsa_out.shape[1]
    dh = q_norm.shape[1]
    n_heads = nsa_w // dh
    kvh = n_heads // NSA_GROUP
    kv_w = kvh * dh
    n_groups = router_group_w.shape[2]
    n_experts = router_expert_w.shape[2]
    n = b * s
    assert rnn_w == COL_BLOCK and sc_w == COL_BLOCK and nsa_w == COL_BLOCK and dh == HEAD_DIM
    assert 4 * kv_w == COL_BLOCK and d % COL_BLOCK == 0
    assert s % COL_BLOCK == 0 and s >= WINDOW + Q_BLOCK and s % SEL_LEN == 0
    assert n_experts + n_groups <= ROUTER_ROWS and n_experts == n_groups * EXPERTS_PER_GROUP

    nrow = s // CMP_STRIDE
    n_sel = s // SEL_LEN
    n_gate = n_heads * N_NSA_BRANCHES
    nsa_cols = nsa_w + 6 * kv_w + n_gate
    c_q = 2 * COL_BLOCK
    c_sc = c_q + nsa_cols
    c_mg = c_sc + 3 * COL_BLOCK
    gate_block = 2 * COL_BLOCK // d
    sc_block = (2 * COL_BLOCK + w_in.shape[2] - c_mg) // COL_BLOCK
    overlap_t = _overlap_t(n_sel, nrow)
    chunk = CMP_STRIDE * dh

    w_main, w_nsa = _pack_w_in(w_in, c_q, c_sc, c_mg)
    w_gate_bf, w_up_bf, w_down_bf = exp_w_gate.astype(BF16), exp_w_up.astype(BF16), exp_w_down.astype(BF16)
    y_moe = None
    for l in range(depth):
        proj3, qt5, kv_rows, v_t, gates_t, x = _inproj(
            x, None if y_moe is None else y_moe.reshape(b, s, d), mix_norm[l][None, :], w_main, w_nsa, l, kvh)

        y_rnn = _rnn_branch(
            proj3, rnn_conv_w[l], rnn_conv_b[l][None, :],
            _block_diag(rg_wa[l]).astype(BF16), rg_ba[l].reshape(1, rnn_w),
            _block_diag(rg_wx[l]).astype(BF16), rg_bx[l].reshape(1, rnn_w), rg_lambda[l][None, :])

        kc_flat = kv_rows[0].reshape(b, kvh, nrow, chunk)
        vc_flat = kv_rows[1].reshape(b, kvh, nrow, chunk)
        pe_flat = cmp_pe[l].reshape(2, 2, chunk)
        k_cmp, v_cmp, k_sel, k_win = _kvprep(
            kc_flat, vc_flat, kv_rows, pe_flat, cmp_w1[l].astype(BF16), cmp_b1[l][:, None, :],
            cmp_w2[l].astype(BF16), cmp_b2[l][:, None, :], k_norm[l])
        y_nsa = _nsa(qt5, k_cmp, v_cmp.transpose(0, 1, 3, 2), k_sel, k_win, v_t, gates_t,
                     q_norm[l][:, None], overlap_t)

        x = _merge(x, proj3, y_rnn, y_nsa, sc_conv_w[l], w_rnn_out[l].astype(BF16), w_nsa_out[l].astype(BF16),
                   w_sc_out[l].astype(BF16), w_out[l].astype(BF16), sc_block, gate_block)

        n_pad = ROUTER_ROWS - n_experts - n_groups
        w_router_t = jnp.concatenate(
            [router_expert_w[l].T, router_group_w[l].T, jnp.zeros((n_pad, d), F32)], axis=0).astype(BF16)
        b_router_t = jnp.concatenate([router_expert_b[l], router_group_b[l], jnp.zeros((n_pad,), F32)])[:, None]
        xn, slots, info_t = _router(x.reshape(n, d), ffn_norm[l][None, :], w_router_t, b_router_t, n_experts, n_groups)
        y_moe = _moe(xn, slots, info_t, w_gate_bf, w_up_bf, w_down_bf, l, n_groups)
    return _residual(x.reshape(n, d), y_moe).reshape(b, s, d)
```

```python
import functools

import jax
import jax.numpy as jnp
from jax import lax
from jax.experimental import pallas as pl
from jax.experimental.pallas import tpu as pltpu

F32 = jnp.float32
BF16 = jnp.bfloat16

RMS_EPS = 1e-6
NEG_INF = -1e30
FORCE_SCORE = 1e6
RG_C = 8.0
HEAD_DIM = 64
NSA_GROUP = 4
N_NSA_BRANCHES = 3
CMP_LEN = 32
CMP_STRIDE = 16
SEL_LEN = 64
SEL_TOPK = 16
WINDOW = 512
Q_BLOCK = 512
EXPERTS_PER_GROUP = 8
TOPK_IN_GROUP = 2

LANES = 128
SUBLANES = 8
COL_BLOCK = 512
VMEM_LIMIT = 56 * 1024 * 1024
LOG2E = 1.4426950408889634


def _cparams(*sem):
    return pltpu.CompilerParams(dimension_semantics=sem, vmem_limit_bytes=VMEM_LIMIT)


def _rms(x, gain):
    return x * lax.rsqrt(jnp.mean(x * x, axis=-1, keepdims=True) + RMS_EPS) * gain


def _gelu_tanh(x):
    return 0.5 * x * (1.0 + jnp.tanh(0.7978845608028654 * (x + 0.044715 * x * x * x)))


def _dot(a, b):
    return jnp.dot(a, b, preferred_element_type=F32)


def _pack_kernel(w_ref, main_ref, nsa_ref, *, c_q, c_sc, c_mg):
    w = w_ref[...]
    n_mg = w.shape[1] - c_mg
    main_ref[:, 0:c_q] = w[:, 0:c_q].astype(BF16)
    main_ref[:, c_q:c_q + n_mg] = w[:, c_mg:].astype(BF16)
    main_ref[:, c_q + n_mg:] = w[:, c_sc:c_mg].astype(BF16)
    nsa_ref[:, 0:c_sc - c_q] = w[:, c_q:c_sc].astype(BF16)
    nsa_ref[:, c_sc - c_q:] = jnp.zeros((w.shape[0], nsa_ref.shape[1] - (c_sc - c_q)), BF16)


def _pack_w_in(w_in, c_q, c_sc, c_mg, tr=128):
    depth, d, win = w_in.shape
    w_main = win - (c_sc - c_q)
    w_nsa = -(-(c_sc - c_q) // LANES) * LANES
    return pl.pallas_call(
        functools.partial(_pack_kernel, c_q=c_q, c_sc=c_sc, c_mg=c_mg),
        grid=(depth, d // tr),
        in_specs=[pl.BlockSpec((None, tr, win), lambda l, i: (l, i, 0))],
        out_specs=[pl.BlockSpec((None, tr, w_main), lambda l, i: (l, i, 0)),
                   pl.BlockSpec((None, tr, w_nsa), lambda l, i: (l, i, 0))],
        out_shape=[jax.ShapeDtypeStruct((depth, d, w_main), BF16), jax.ShapeDtypeStruct((depth, d, w_nsa), BF16)],
        compiler_params=_cparams("parallel", "parallel"),
        name="pack_w_in",
    )(w_in)


def _inproj_kernel(*refs, has_y, kvh):
    if has_y:
        x_ref, y_ref, g_ref, w_ref, wn_ref, o_ref, qt_ref, kf_ref, kr_ref, vt_ref, gt_ref, xo_ref = refs[:-2]
        x = x_ref[...] + y_ref[...].astype(F32)
        xo_ref[...] = x
    else:
        x_ref, g_ref, w_ref, wn_ref, o_ref, qt_ref, kf_ref, kr_ref, vt_ref, gt_ref = refs[:-2]
        x = x_ref[...]
    xn = _rms(x, g_ref[...]).astype(BF16)
    for c in range(w_ref.shape[1] // COL_BLOCK):
        sl = slice(c * COL_BLOCK, (c + 1) * COL_BLOCK)
        o_ref[:, sl] = _dot(xn, w_ref[:, sl]).astype(BF16)

    tm = x.shape[0]
    dh = HEAD_DIM
    tq = Q_BLOCK
    nsa = _dot(xn, wn_ref[...])
    for pair in range(kvh * NSA_GROUP // 2):
        slab_t = nsa[:, pair * LANES:(pair + 1) * LANES].T
        for half in range(2):
            head = 2 * pair + half
            g, h = head // NSA_GROUP, head % NSA_GROUP
            for j in range(tm // tq):
                qt_ref[g, j, :, h * tq:(h + 1) * tq] = slab_t[half * dh:(half + 1) * dh,
                                                             j * tq:(j + 1) * tq].astype(BF16)
    kv0 = kvh * NSA_GROUP * dh
    ones_rows = jnp.where(lax.broadcasted_iota(jnp.int32, (2 * SUBLANES, tm), 0) == 0, 1.0, 0.0).astype(BF16)
    per_tile = LANES // dh
    for slab, (row_slot, col_slot) in enumerate(((None, None), (None, None), (0, None), (None, 0), (1, None), (None, 1))):
        v = nsa[:, kv0 + slab * kvh * dh:kv0 + (slab + 1) * kvh * dh]
        if row_slot is None and col_slot is None:
            stage = refs[-2 + slab]
            stage[...] = v
            for tile in range(CMP_STRIDE // per_tile):
                toks = [stage[pl.ds(tile * per_tile + k, tm // CMP_STRIDE, stride=CMP_STRIDE), :]
                        for k in range(per_tile)]
                for g in range(kvh):
                    kf_ref[slab, g, :, tile * LANES:(tile + 1) * LANES] = jnp.concatenate(
                        [t[:, g * dh:(g + 1) * dh] for t in toks], axis=1).astype(BF16)
        elif row_slot is not None:
            for g in range(kvh):
                kr_ref[row_slot, g] = v[:, g * dh:(g + 1) * dh].astype(BF16)
        else:
            v_t = v.T
            for g in range(kvh):
                vt_ref[col_slot, g, 0:dh, :] = v_t[g * dh:(g + 1) * dh].astype(BF16)
                vt_ref[col_slot, g, dh:dh + 2 * SUBLANES, :] = ones_rows
    g0 = kv0 + 6 * kvh * dh
    gates_t = nsa[:, g0:g0 + LANES].T
    per_group = NSA_GROUP * N_NSA_BRANCHES
    for g in range(kvh):
        gt_ref[g] = gates_t[g * per_group:(g + 1) * per_group].astype(BF16)


def _inproj(x3, y3, gain, w_main, w_nsa, layer, kvh, tm=512):
    b, s, d = x3.shape
    wp = w_main.shape[2]
    dh = HEAD_DIM
    row = pl.BlockSpec((None, tm, d), lambda bi, ti: (bi, ti, 0))
    has_y = y3 is not None
    lanes = NSA_GROUP * Q_BLOCK
    per_group = NSA_GROUP * N_NSA_BRANCHES
    out_specs = [
        pl.BlockSpec((None, tm, wp), lambda bi, ti: (bi, ti, 0)),
        pl.BlockSpec((None, kvh, tm // Q_BLOCK, dh, lanes), lambda bi, ti: (bi, 0, ti, 0, 0)),
        pl.BlockSpec((2, None, kvh, tm // CMP_STRIDE, CMP_STRIDE * dh), lambda bi, ti: (0, bi, 0, ti, 0)),
        pl.BlockSpec((2, None, kvh, tm, dh), lambda bi, ti: (0, bi, 0, ti, 0)),
        pl.BlockSpec((2, None, kvh, dh + 2 * SUBLANES, tm), lambda bi, ti: (0, bi, 0, 0, ti)),
        pl.BlockSpec((None, kvh, per_group, tm), lambda bi, ti: (bi, 0, 0, ti)),
    ] + ([row] if has_y else [])
    out_shape = [
        jax.ShapeDtypeStruct((b, s, wp), BF16),
        jax.ShapeDtypeStruct((b, kvh, s // Q_BLOCK, dh, lanes), BF16),
        jax.ShapeDtypeStruct((2, b, kvh, s // CMP_STRIDE, CMP_STRIDE * dh), BF16),
        jax.ShapeDtypeStruct((2, b, kvh, s, dh), BF16),
        jax.ShapeDtypeStruct((2, b, kvh, dh + 2 * SUBLANES, s), BF16),
        jax.ShapeDtypeStruct((b, kvh, per_group, s), BF16),
    ] + ([jax.ShapeDtypeStruct((b, s, d), F32)] if has_y else [])
    const = lambda a: pl.BlockSpec((None,) + a.shape[1:], lambda bi, ti: (layer, 0, 0), pipeline_mode=pl.Buffered(1))
    res = pl.pallas_call(
        functools.partial(_inproj_kernel, has_y=has_y, kvh=kvh),
        grid=(b, s // tm),
        in_specs=[row] + ([row] if has_y else []) + [pl.BlockSpec((1, d), lambda bi, ti: (0, 0)),
                                                    const(w_main), const(w_nsa)],
        out_specs=out_specs,
        out_shape=out_shape,
        scratch_shapes=[pltpu.VMEM((tm, kvh * dh), F32), pltpu.VMEM((tm, kvh * dh), F32)],
        compiler_params=_cparams("parallel", "parallel"),
        name="inproj",
    )(*([x3, y3] if has_y else [x3]), gain, w_main, w_nsa)
    return tuple(res[:6]) + ((res[6],) if has_y else (x3,))


def _rnn_kernel(x_ref, xp_ref, y_ref, cw_ref, cb_ref, wa_ref, ba_ref, wx_ref, bx_ref, lam_ref,
                o_ref, h_scr):
    ti = pl.program_id(1)
    ts = x_ref.shape[0]
    kw = cw_ref.shape[0]

    @pl.when(ti == 0)
    def _():
        h_scr[...] = jnp.zeros_like(h_scr)

    x = x_ref[...].astype(F32)
    halo = jnp.where(ti == 0, 0.0, xp_ref[...].astype(F32))
    groups_x = jnp.concatenate([halo, x], axis=0).reshape(ts // SUBLANES + 1, SUBLANES, x.shape[1])
    row_in = lax.broadcasted_iota(jnp.int32, (ts // SUBLANES, SUBLANES, x.shape[1]), 1)
    u = cb_ref[...] + cw_ref[kw - 1:kw, :] * groups_x[1:]
    for shift in range(1, kw):
        rot = pltpu.roll(groups_x, shift, axis=1)
        u = u + cw_ref[kw - 1 - shift:kw - shift, :] * jnp.where(row_in >= shift, rot[1:], rot[:-1])
    u = u.reshape(ts, x.shape[1])
    ub = u.astype(BF16)
    r = jax.nn.sigmoid(_dot(ub, wa_ref[...]) + ba_ref[...])
    gi = jax.nn.sigmoid(_dot(ub, wx_ref[...]) + bx_ref[...])
    lam = lam_ref[...]
    softplus_neg = jnp.maximum(-lam, 0.0) + jnp.log(1.0 + jnp.exp(-jnp.abs(lam)))
    log_a = -RG_C * r * softplus_neg
    a = jnp.exp(log_a)
    b = jnp.sqrt(1.0 - a * a) * (gi * u)

    a = a.reshape(ts // SUBLANES, SUBLANES, a.shape[1])
    b = b.reshape(a.shape)
    row_in_group = lax.broadcasted_iota(jnp.int32, a.shape, 1)
    d = 1
    while d < SUBLANES:
        keep = row_in_group >= d
        a_sh = pltpu.roll(a, d, axis=1)
        b_sh = pltpu.roll(b, d, axis=1)
        b = jnp.where(keep, a * b_sh + b, b)
        a = jnp.where(keep, a * a_sh, a)
        d *= 2
    h_prev = h_scr[...]
    groups = []
    for r in range(ts // SUBLANES):
        h_r = a[r] * h_prev + b[r]
        groups.append(h_r)
        h_prev = h_r[SUBLANES - 1:SUBLANES, :]
    h = jnp.concatenate(groups, axis=0)
    h_scr[...] = h_prev
    o_ref[...] = (_gelu_tanh(y_ref[...].astype(F32)) * h).astype(BF16)


def _rnn_branch(proj3, conv_w, conv_b, wa_bd, ba, wx_bd, bx, lam, ts=512):
    b, s, _ = proj3.shape
    w = conv_w.shape[1]
    nb = w // COL_BLOCK
    rows_per = ts // SUBLANES
    const = lambda shape: pl.BlockSpec(shape, lambda bi, ti: (0, 0))
    return pl.pallas_call(
        _rnn_kernel,
        grid=(b, s // ts),
        in_specs=[
            pl.BlockSpec((None, ts, w), lambda bi, ti: (bi, ti, 0)),
            pl.BlockSpec((None, SUBLANES, w), lambda bi, ti: (bi, jnp.maximum(ti * rows_per - 1, 0), 0)),
            pl.BlockSpec((None, ts, w), lambda bi, ti: (bi, ti, nb)),
            const(conv_w.shape), const((1, w)), const((w, w)), const((1, w)), const((w, w)), const((1, w)),
            const((1, w)),
        ],
        out_specs=pl.BlockSpec((None, ts, w), lambda bi, ti: (bi, ti, 0)),
        out_shape=jax.ShapeDtypeStruct((b, s, w), BF16),
        scratch_shapes=[pltpu.VMEM((1, w), F32)],
        compiler_params=_cparams("parallel", "arbitrary"),
        name="rnn_branch",
    )(proj3, proj3, proj3, conv_w, conv_b, wa_bd, ba, wx_bd, bx, lam)


def _kvprep_kernel(kcf_ref, vcf_ref, ks_ref, kw_ref, pe_ref, w1_ref, b1_ref, w2_ref, b2_ref, kn_ref,
                   kc_o, vc_o, ks_o, kw_o):
    half = kcf_ref.shape[1]
    nrow = kcf_ref.shape[0]

    def compress(flat_ref, i):
        xf = flat_ref[...].astype(F32)
        xa = (xf + pe_ref[i, 0:1, :]).astype(BF16)
        xb = (xf + pe_ref[i, 1:2, :]).astype(BF16)
        first = _dot(xa, w1_ref[i, 0:half, :])
        second = _dot(xb, w1_ref[i, half:2 * half, :])
        hid = first + pltpu.roll(second, nrow - 1, axis=0) + b1_ref[i]
        hid = _gelu_tanh(hid).astype(BF16)
        return _dot(hid, w2_ref[i]) + b2_ref[i]

    kc_o[...] = _rms(compress(kcf_ref, 0), kn_ref[0:1, :]).astype(BF16)
    vc_o[...] = compress(vcf_ref, 1).astype(BF16)
    ks = _rms(ks_ref[...].astype(F32), kn_ref[1:2, :])
    row = lax.broadcasted_iota(jnp.int32, ks.shape, 0)
    col = lax.broadcasted_iota(jnp.int32, ks.shape, 1)
    block_hot = jnp.where((row // SEL_LEN) % SUBLANES == col, 1.0, 0.0)
    ks_o[...] = jnp.concatenate([ks, block_hot], axis=1).astype(BF16)
    kw_o[...] = _rms(kw_ref[...].astype(F32), kn_ref[2:3, :]).astype(BF16)


def _kvprep(kv_flat, kv_rows, pe_flat, w1, b1, w2, b2, k_norm):
    _, b, g, nrow, half = kv_flat.shape
    flat = lambda k: pl.BlockSpec((None, None, None, nrow, half), lambda bi, gi: (k, bi, gi, 0, 0))
    s = kv_rows.shape[3]
    dh = kv_rows.shape[4]
    slot = lambda k: pl.BlockSpec((None, None, None, s, dh), lambda bi, gi: (k, bi, gi, 0, 0))
    bg = lambda shape: pl.BlockSpec((None, None) + shape, lambda bi, gi: (bi, gi, 0, 0))
    full = lambda a: pl.BlockSpec(a.shape, lambda bi, gi: (0,) * a.ndim)
    return pl.pallas_call(
        _kvprep_kernel,
        grid=(b, g),
        in_specs=[flat(0), flat(1), slot(0), slot(1),
                  full(pe_flat), full(w1), full(b1), full(w2), full(b2), full(k_norm)],
        out_specs=[bg((nrow, dh)), bg((nrow, dh)), bg((s, 2 * dh)), bg((s, dh))],
        out_shape=[jax.ShapeDtypeStruct((b, g, nrow, dh), BF16), jax.ShapeDtypeStruct((b, g, nrow, dh), BF16),
                   jax.ShapeDtypeStruct((b, g, s, 2 * dh), BF16), jax.ShapeDtypeStruct((b, g, s, dh), BF16)],
        compiler_params=_cparams("parallel", "parallel"),
        name="kvprep",
    )(kv_flat, kv_flat, kv_rows, kv_rows, pe_flat, w1, b1, w2, b2, k_norm)


def _nsa_kernel(qt_ref, kc_ref, vct_ref, ks_ref, vst_ref, kw_ref, vwt_ref, gt_ref, qn_ref, ov_ref, cb_ref, bb_ref,
                o_ref, sel_scr, *, key_tile):
    c = pl.program_id(2)
    tq = Q_BLOCK
    t0 = c * tq
    n_sel = ov_ref.shape[0]
    ncp = kc_ref.shape[0]
    top_k = min(SEL_TOPK, n_sel)
    lanes = NSA_GROUP * tq
    blocks_per_tile = key_tile // SEL_LEN

    qf = qt_ref[...].astype(F32)
    qn = qf * lax.rsqrt(jnp.mean(qf * qf, axis=0, keepdims=True) + RMS_EPS) * qn_ref[...]
    qs = (qn * (HEAD_DIM ** -0.5 * LOG2E)).astype(BF16)
    t_lane = t0 + (lax.broadcasted_iota(jnp.int32, (1, lanes), 1) & (tq - 1))

    n_tiles = (t0 + tq + key_tile - 1) // key_tile
    n_full = n_tiles - 1

    pad_rows = ks_ref.shape[1] - HEAD_DIM - 2 * SUBLANES
    assert blocks_per_tile == SUBLANES and pad_rows >= 0

    def scores(kt, bias_rows=None):
        k0 = pl.multiple_of(kt * key_tile, key_tile)
        if bias_rows is None:
            bias_rows = jnp.zeros((SUBLANES, lanes), F32)
        extra = jnp.concatenate([bias_rows, jnp.zeros((SUBLANES, lanes), F32)], axis=0).astype(BF16)
        q_aug = jnp.concatenate([qs, extra, jnp.zeros((pad_rows, lanes), BF16)], axis=0)
        return _dot(ks_ref[pl.ds(k0, key_tile), :], q_aug)

    wl = WINDOW + tq
    w0 = pl.multiple_of(jnp.maximum(t0 - WINDOW, 0), tq)
    s_c = _dot(kc_ref[...], qs)
    s_w = _dot(kw_ref[pl.ds(w0, wl), :], qs)
    s_l = scores(n_full)

    off_c = pl.multiple_of(cb_ref.shape[0] - ncp - t0 // CMP_STRIDE, SUBLANES)
    s_c = s_c + cb_ref[pl.ds(off_c, ncp), :]
    e_c = jnp.exp2(s_c - jnp.max(s_c, axis=0, keepdims=True))
    inv_c = 1.0 / jnp.sum(e_c, axis=0, keepdims=True)
    p_c = e_c * jnp.where(t_lane >= CMP_LEN - 1, inv_c, 0.0)
    o_c = _dot(vct_ref[...], p_c.astype(BF16))

    j_i = lax.broadcasted_iota(jnp.int32, (n_sel, tq), 0)
    cur = (t0 + lax.broadcasted_iota(jnp.int32, (n_sel, tq), 1)) // SEL_LEN
    ps = p_c[:, 0:tq]
    for h in range(1, NSA_GROUP):
        ps = ps + p_c[:, h * tq:(h + 1) * tq]
    ps_hi = ps.astype(BF16)
    ps_lo = (ps - ps_hi.astype(F32)).astype(BF16)
    imp = _dot(ov_ref[...], ps_hi) + _dot(ov_ref[...], ps_lo)
    forced = (j_i == 0) | (j_i == cur) | (j_i == cur - 1)
    imp = jnp.where(forced, FORCE_SCORE, imp)
    imp = jnp.where(j_i <= cur, imp, -1.0)
    slabs = [imp[v * SUBLANES:(v + 1) * SUBLANES, :] for v in range(n_sel // SUBLANES)]
    j_slab = lax.broadcasted_iota(jnp.int32, (SUBLANES, tq), 0)
    cnts = [jnp.zeros((SUBLANES, tq), jnp.int32) for _ in slabs]
    for jp in range(n_sel):
        r = imp[jp:jp + 1, :]
        for v, slab in enumerate(slabs):
            lo = v * SUBLANES
            if lo > jp:
                before = r >= slab
            elif lo + SUBLANES - 1 <= jp:
                before = r > slab
            else:
                before = (r > slab) | ((r == slab) & (j_slab + lo > jp))
            cnts[v] = cnts[v] + before.astype(jnp.int32)
    cnt = jnp.concatenate(cnts, axis=0)
    sel_bias = jnp.where((cnt < top_k) & (imp >= 0.0), 0.0, NEG_INF)
    sel_scr[...] = jnp.concatenate([sel_bias] * NSA_GROUP, axis=1)

    off_w = pl.multiple_of(WINDOW - (t0 - w0), tq)
    s_w = s_w + bb_ref[pl.ds(off_w, wl), :]
    e_w = jnp.exp2(s_w - jnp.max(s_w, axis=0, keepdims=True))
    a_w = _dot(vwt_ref[:, pl.ds(w0, wl)], e_w.astype(BF16))
    o_w = a_w[0:HEAD_DIM] * (1.0 / a_w[HEAD_DIM:HEAD_DIM + 1])

    def bias_rows(kt):
        return sel_scr[pl.ds(pl.multiple_of(kt * blocks_per_tile, SUBLANES), blocks_per_tile), :]

    def update(carry, kt, s):
        m_o, a_o = carry
        k0 = pl.multiple_of(kt * key_tile, key_tile)
        m_n = jnp.maximum(m_o, jnp.max(s, axis=0, keepdims=True))
        p = jnp.exp2(s - m_n).astype(BF16)
        a_n = jnp.exp2(m_o - m_n) * a_o + _dot(vst_ref[:, pl.ds(k0, key_tile)], p)
        return m_n, a_n

    def init():
        return jnp.full((1, lanes), NEG_INF, F32), jnp.zeros((vst_ref.shape[0], lanes), F32)

    off_l = pl.multiple_of(WINDOW - (t0 - n_full * key_tile), tq)
    last_rows = bias_rows(n_full)
    last_bias = jnp.concatenate(
        [jnp.broadcast_to(last_rows[jb:jb + 1, :], (SEL_LEN, lanes)) for jb in range(blocks_per_tile)], axis=0)
    c_first = update(init(), n_full, s_l + last_bias + bb_ref[pl.ds(off_l, key_tile), :])

    def pair_step(i, carry):
        c_a, c_b = carry
        s_a = scores(2 * i, bias_rows(2 * i))
        s_b = scores(2 * i + 1, bias_rows(2 * i + 1))
        return update(c_a, 2 * i, s_a), update(c_b, 2 * i + 1, s_b)

    c_a, c_b = lax.fori_loop(0, n_full // 2, pair_step, (c_first, init()))
    (m_a, a_a), (m_b, a_b) = c_a, lax.cond(
        n_full % 2 == 1,
        lambda c: update(c, n_full - 1, scores(n_full - 1, bias_rows(n_full - 1))),
        lambda c: c, c_b)
    m_ab = jnp.maximum(m_a, m_b)
    a_s = jnp.exp2(m_a - m_ab) * a_a + jnp.exp2(m_b - m_ab) * a_b
    o_s = a_s[0:HEAD_DIM] * (1.0 / a_s[HEAD_DIM:HEAD_DIM + 1])

    gates = jax.nn.sigmoid(gt_ref[...].astype(F32))
    for h in range(NSA_GROUP):
        hs = slice(h * tq, (h + 1) * tq)
        gb = h * N_NSA_BRANCHES
        o = (gates[gb:gb + 1, :] * o_c[:, hs] + gates[gb + 1:gb + 2, :] * o_s[:, hs]
             + gates[gb + 2:gb + 3, :] * o_w[:, hs])
        o_ref[:, h * HEAD_DIM:(h + 1) * HEAD_DIM] = o.T.astype(BF16)


def _band_bias(n_rows, lanes):
    i = jnp.arange(n_rows)[:, None]
    tl = (jnp.arange(lanes) % Q_BLOCK)[None, :]
    return jnp.where((tl < i) & (i <= tl + WINDOW), 0.0, NEG_INF).astype(F32)


def _cmp_bias(n_lead, ncp, lanes):
    i = jnp.arange(n_lead + ncp)[:, None]
    tl = (jnp.arange(lanes) % Q_BLOCK)[None, :]
    return jnp.where(CMP_STRIDE * (i - n_lead) + CMP_LEN - 1 <= tl, 0.0, NEG_INF).astype(F32)


def _nsa(qt5, k_cmp, v_cmp_t, k_sel, k_win, v_t, gates_t, q_norm_col, overlap_t, key_tile=512):
    b, g, nchunk, dh, lanes = qt5.shape
    s = k_sel.shape[2]
    ncp = k_cmp.shape[2]
    ng = gates_t.shape[2]
    n_sel = overlap_t.shape[0]
    assert key_tile <= WINDOW and key_tile % Q_BLOCK == 0 and s % key_tile == 0
    cmp_bias = _cmp_bias((s - Q_BLOCK) // CMP_STRIDE, ncp, lanes)
    band_bias = _band_bias(2 * WINDOW + Q_BLOCK, lanes)
    bg = lambda shape: pl.BlockSpec((None, None) + shape, lambda bi, gi, ci: (bi, gi, 0, 0))
    v_slot = lambda k: pl.BlockSpec((None, None, None) + v_t.shape[3:], lambda bi, gi, ci: (k, bi, gi, 0, 0))
    const = lambda a: pl.BlockSpec(a.shape, lambda bi, gi, ci: (0, 0), pipeline_mode=pl.Buffered(1))
    return pl.pallas_call(
        functools.partial(_nsa_kernel, key_tile=key_tile),
        grid=(b, g, nchunk),
        in_specs=[
            pl.BlockSpec((None, None, None, dh, lanes), lambda bi, gi, ci: (bi, gi, ci, 0, 0)),
            bg((ncp, dh)), bg((dh, ncp)), bg((s, k_sel.shape[3])), v_slot(0), bg((s, dh)), v_slot(1),
            pl.BlockSpec((None, None, ng, Q_BLOCK), lambda bi, gi, ci: (bi, gi, 0, ci)),
            pl.BlockSpec(q_norm_col.shape, lambda bi, gi, ci: (0, 0)),
            pl.BlockSpec(overlap_t.shape, lambda bi, gi, ci: (0, 0)),
            const(cmp_bias), const(band_bias),
        ],
        out_specs=pl.BlockSpec((None, Q_BLOCK, NSA_GROUP * dh), lambda bi, gi, ci: (bi, ci, gi)),
        out_shape=jax.ShapeDtypeStruct((b, s, g * NSA_GROUP * dh), BF16),
        scratch_shapes=[pltpu.VMEM((n_sel, lanes), F32)],
        compiler_params=_cparams("parallel", "parallel", "arbitrary"),
        name="nsa",
    )(qt5, k_cmp, v_cmp_t, k_sel, v_t, k_win, v_t, gates_t, q_norm_col, overlap_t, cmp_bias, band_bias)


def _merge_kernel(x_ref, yr_ref, yn_ref, sb_ref, sc_ref, sh_ref, scp_ref, shp_ref, g0_ref, g1_ref, g2_ref,
                  cw_ref, wr_ref, wn_ref, ws_ref, wo_ref, o_ref):
    ti = pl.program_id(1)
    tm = x_ref.shape[0]
    kw = cw_ref.shape[0]
    prod = sc_ref[...].astype(F32) * sh_ref[...].astype(F32)
    halo = jnp.where(ti == 0, 0.0, scp_ref[...].astype(F32) * shp_ref[...].astype(F32))
    groups_p = jnp.concatenate([halo, prod], axis=0).reshape(tm // SUBLANES + 1, SUBLANES, prod.shape[1])
    row_in = lax.broadcasted_iota(jnp.int32, (tm // SUBLANES, SUBLANES, prod.shape[1]), 1)
    conv = cw_ref[kw - 1:kw, :] * groups_p[1:]
    for shift in range(1, kw):
        rot = pltpu.roll(groups_p, shift, axis=1)
        conv = conv + cw_ref[kw - 1 - shift:kw - shift, :] * jnp.where(row_in >= shift, rot[1:], rot[:-1])
    y_sc = (sb_ref[...].astype(F32) * conv.reshape(tm, prod.shape[1])).astype(BF16)
    merged = (jax.nn.sigmoid(g0_ref[...].astype(F32)) * _dot(yr_ref[...], wr_ref[...])
              + jax.nn.sigmoid(g1_ref[...].astype(F32)) * _dot(yn_ref[...], wn_ref[...])
              + jax.nn.sigmoid(g2_ref[...].astype(F32)) * _dot(y_sc, ws_ref[...]))
    o_ref[...] = x_ref[...] + _dot(merged.astype(BF16), wo_ref[...])


def _merge(x3, proj3, y_rnn, y_nsa, sc_conv_w, w_rnn_out, w_nsa_out, w_sc_out, w_out, sc_block, gate_block, tm=512):
    b, s, d = x3.shape
    w = y_rnn.shape[2]
    rows_per = tm // SUBLANES
    tile = lambda width, col: pl.BlockSpec((None, tm, width), lambda bi, ti: (bi, ti, col))
    prev = lambda col: pl.BlockSpec((None, SUBLANES, w),
                                    lambda bi, ti: (bi, jnp.maximum(ti * rows_per - 1, 0), col))
    const = lambda a: pl.BlockSpec(a.shape, lambda bi, ti: (0, 0))
    return pl.pallas_call(
        _merge_kernel,
        grid=(b, s // tm),
        in_specs=[
            tile(d, 0), tile(w, 0), tile(w, 0),
            tile(w, sc_block), tile(w, sc_block + 1), tile(w, sc_block + 2),
            prev(sc_block + 1), prev(sc_block + 2),
            tile(d, gate_block), tile(d, gate_block + 1), tile(d, gate_block + 2),
            const(sc_conv_w), const(w_rnn_out), const(w_nsa_out), const(w_sc_out), const(w_out),
        ],
        out_specs=tile(d, 0),
        out_shape=jax.ShapeDtypeStruct((b, s, d), F32),
        compiler_params=_cparams("parallel", "parallel"),
        name="merge",
    )(x3, y_rnn, y_nsa, proj3, proj3, proj3, proj3, proj3, proj3, proj3, proj3,
      sc_conv_w, w_rnn_out, w_nsa_out, w_sc_out, w_out)


MOE_BLOCK = 1024
MOE_ROW_TILE = 304
SLOT_IND = EXPERTS_PER_GROUP
SLOT_POS = EXPERTS_PER_GROUP + 1


ROUTER_ROWS = 48


def _route_t(logits, n_experts, n_groups):
    row_i = lax.broadcasted_iota(jnp.int32, logits.shape, 0)
    row = row_i.astype(F32)
    big = float(logits.shape[0])
    is_g = (row_i >= n_experts) & (row_i < n_experts + n_groups)
    gl = jnp.where(is_g, logits, NEG_INF)
    gmax = jnp.max(gl, axis=0, keepdims=True)
    grp = jnp.min(jnp.where(is_g & (gl == gmax), row - n_experts, big), axis=0, keepdims=True)
    p_grp = 1.0 / jnp.sum(jnp.where(is_g, jnp.exp(gl - gmax), 0.0), axis=0, keepdims=True)
    in_g = (row_i < n_experts) & ((row_i // EXPERTS_PER_GROUP).astype(F32) == grp)
    el = jnp.where(in_g, logits, NEG_INF)
    ee = jnp.where(in_g, jnp.exp(el - jnp.max(el, axis=0, keepdims=True)), 0.0)
    ep = ee / jnp.sum(ee, axis=0, keepdims=True)
    top1 = jnp.max(jnp.where(in_g, ep, -1.0), axis=0, keepdims=True)
    i1 = jnp.min(jnp.where(in_g & (ep == top1), row, big), axis=0, keepdims=True)
    rest = jnp.where(in_g & (row != i1), ep, -1.0)
    top2 = jnp.max(rest, axis=0, keepdims=True)
    i2 = jnp.min(jnp.where(rest == top2, row, big), axis=0, keepdims=True)
    denom = top1 + top2
    cw = (jnp.where(row == i1, p_grp * top1 / denom, 0.0)
          + jnp.where(row == i2, p_grp * top2 / denom, 0.0))
    return cw, grp


def _router_kernel(x_ref, g_ref, wr_ref, br_ref, tri_ref, xn_ref, slot_ref, info_t_ref, *, n_experts, n_groups):
    xn = _rms(x_ref[...], g_ref[...]).astype(BF16)
    xn_ref[...] = xn
    tb = xn.shape[0]
    logits = lax.dot_general(wr_ref[...], xn, (((1,), (1,)), ((), ())), preferred_element_type=F32) + br_ref[...]
    cw, grp = _route_t(logits, n_experts, n_groups)
    g_row = lax.broadcasted_iota(jnp.int32, (SUBLANES, tb), 0)
    ind = jnp.where((g_row.astype(F32) == grp) & (g_row < n_groups), 1.0, 0.0)
    pos = _dot(ind.astype(BF16), tri_ref[...])
    info_t_ref[...] = jnp.where(g_row < n_groups, ind, pltpu.roll(pos, n_groups, axis=0))
    pad = jnp.zeros((LANES - 2 * SUBLANES, tb), F32)
    extra_row = lax.broadcasted_iota(jnp.int32, (SUBLANES, tb), 0)
    for g in range(n_groups):
        w_g = cw[g * EXPERTS_PER_GROUP:(g + 1) * EXPERTS_PER_GROUP]
        extra = jnp.where(extra_row == SLOT_IND - EXPERTS_PER_GROUP, ind[g:g + 1],
                          jnp.where(extra_row == SLOT_POS - EXPERTS_PER_GROUP, pos[g:g + 1], 0.0))
        slot_ref[:, g * LANES:(g + 1) * LANES] = jnp.concatenate([w_g, extra, pad], axis=0).T


def _router(x2, gain, w_router_t, b_router_t, n_experts, n_groups):
    n, d = x2.shape
    tb = MOE_BLOCK
    assert EXPERTS_PER_GROUP == SUBLANES and n_groups <= SUBLANES // 2
    tri = (jnp.arange(tb)[:, None] < jnp.arange(tb)[None, :]).astype(BF16)
    const = lambda a: pl.BlockSpec(a.shape, lambda i: (0, 0))
    return pl.pallas_call(
        functools.partial(_router_kernel, n_experts=n_experts, n_groups=n_groups),
        grid=(n // tb,),
        in_specs=[pl.BlockSpec((tb, d), lambda i: (i, 0)), const(gain), const(w_router_t), const(b_router_t),
                  pl.BlockSpec(tri.shape, lambda i: (0, 0), pipeline_mode=pl.Buffered(1))],
        out_specs=[pl.BlockSpec((tb, d), lambda i: (i, 0)),
                   pl.BlockSpec((tb, n_groups * LANES), lambda i: (i, 0)),
                   pl.BlockSpec((2 * n_groups, tb), lambda i: (0, i))],
        out_shape=[jax.ShapeDtypeStruct((n, d), BF16),
                   jax.ShapeDtypeStruct((n, n_groups * LANES), F32),
                   jax.ShapeDtypeStruct((2 * n_groups, n), F32)],
        compiler_params=_cparams("parallel"),
        name="router",
    )(x2, gain, w_router_t, b_router_t, tri)


def _moe_kernel(*refs, n_groups, has_x):
    if has_x:
        xn_ref, slot_ref, info_t_ref, wg_ref, wu_ref, wd_ref, x_ref, o_ref = refs
    else:
        xn_ref, slot_ref, info_t_ref, wg_ref, wu_ref, wd_ref, o_ref = refs
    g = pl.program_id(1)
    tb = xn_ref.shape[0]
    rt = MOE_ROW_TILE

    @pl.when(g == 0)
    def _():
        o_ref[...] = x_ref[...] if has_x else jnp.zeros_like(o_ref)

    slot = slot_ref[...]
    ind_col = slot[:, SLOT_IND:SLOT_IND + 1] > 0.5
    pos_col = slot[:, SLOT_POS:SLOT_POS + 1]
    ind_row = info_t_ref[pl.ds(g, 1), :]
    pos_row = info_t_ref[pl.ds(n_groups + g, 1), :]
    n_rows = jnp.sum(ind_row).astype(jnp.int32)
    n_tiles = (n_rows + rt - 1) // rt
    slot_hi = slot.astype(BF16)
    slot_lo = (slot - slot_hi.astype(F32)).astype(BF16)
    row_id = lax.broadcasted_iota(jnp.int32, (rt, tb), 0).astype(F32)
    col_id = lax.broadcasted_iota(jnp.int32, (tb, rt), 1).astype(F32)

    def tile(t, _):
        base = (t * rt).astype(F32)
        pick = jnp.where((pos_row == row_id + base) & (ind_row > 0.5), 1.0, 0.0).astype(BF16)
        xg = _dot(pick, xn_ref[...]).astype(BF16)
        wc = _dot(pick, slot_hi) + _dot(pick, slot_lo)
        y = None
        for e in range(EXPERTS_PER_GROUP):
            h = jax.nn.silu(_dot(xg, wg_ref[e])) * _dot(xg, wu_ref[e]) * wc[:, e:e + 1]
            y_e = _dot(h.astype(BF16), wd_ref[e])
            y = y_e if y is None else y + y_e
        put = jnp.where((pos_col == col_id + base) & ind_col, 1.0, 0.0).astype(BF16)
        o_ref[...] += _dot(put, y.astype(BF16)).astype(o_ref.dtype)
        return 0

    lax.fori_loop(0, n_tiles, tile, 0)


def _moe(xn, slots, info_t, w_gate, w_up, w_down, layer, n_groups, x_res=None):
    n, d = xn.shape
    f = w_down.shape[2]
    tb = MOE_BLOCK
    epg = EXPERTS_PER_GROUP
    assert MOE_ROW_TILE % (2 * SUBLANES) == 0
    has_x = x_res is not None
    row = pl.BlockSpec((tb, d), lambda i, g: (i, 0))
    return pl.pallas_call(
        functools.partial(_moe_kernel, n_groups=n_groups, has_x=has_x),
        grid=(n // tb, n_groups),
        in_specs=[
            pl.BlockSpec((tb, d), lambda i, g: (i, 0)),
            pl.BlockSpec((tb, LANES), lambda i, g: (i, g)),
            pl.BlockSpec((2 * n_groups, tb), lambda i, g: (0, i)),
            pl.BlockSpec((None, epg, d, f), lambda i, g: (layer, g, 0, 0)),
            pl.BlockSpec((None, epg, d, f), lambda i, g: (layer, g, 0, 0)),
            pl.BlockSpec((None, epg, f, d), lambda i, g: (layer, g, 0, 0)),
        ] + ([row] if has_x else []),
        out_specs=row,
        out_shape=jax.ShapeDtypeStruct((n, d), F32 if has_x else BF16),
        compiler_params=_cparams("parallel", "arbitrary"),
        name="moe",
    )(xn, slots, info_t, w_gate, w_up, w_down, *([x_res] if has_x else []))


def _block_diag(w):
    nb, bs, _ = w.shape
    eye = jnp.eye(nb, dtype=w.dtype)
    return (eye[:, None, :, None] * w[:, :, None, :]).reshape(nb * bs, nb * bs)


def _overlap_t(n_sel, ncp):
    n = jnp.arange(ncp)
    j = jnp.arange(n_sel)
    c0 = n * CMP_STRIDE
    s0 = j * SEL_LEN
    ov = jnp.clip(jnp.minimum(c0[None, :] + CMP_LEN, s0[:, None] + SEL_LEN)
                  - jnp.maximum(c0[None, :], s0[:, None]), 0).astype(F32) / CMP_LEN
    return ov.astype(BF16)


def kernel(x, mix_norm, w_in, rnn_conv_w, rnn_conv_b, rg_wa, rg_ba, rg_wx, rg_bx, rg_lambda, cmp_pe, cmp_w1, cmp_b1, cmp_w2, cmp_b2, q_norm, k_norm, sc_conv_w, w_rnn_out, w_nsa_out, w_sc_out, w_out, ffn_norm, router_group_w, router_group_b, router_expert_w, router_expert_b, exp_w_gate, exp_w_up, exp_w_down):
    b, s, d = x.shape
    depth = w_in.shape[0]
    rnn_w = rnn_conv_w.shape[2]
    sc_w = sc_conv_w.shape[2]
    nsa_w = w_nsa_out.shape[1]
    dh = q_norm.shape[1]
    n_heads = nsa_w // dh
    kvh = n_heads // NSA_GROUP
    kv_w = kvh * dh
    n_groups = router_group_w.shape[2]
    n_experts = router_expert_w.shape[2]
    n = b * s
    assert rnn_w == COL_BLOCK and sc_w == COL_BLOCK and nsa_w == COL_BLOCK and dh == HEAD_DIM
    assert 4 * kv_w == COL_BLOCK and d % COL_BLOCK == 0
    assert s % COL_BLOCK == 0 and s >= WINDOW + Q_BLOCK and s % SEL_LEN == 0
    assert n_experts + n_groups <= ROUTER_ROWS and n_experts == n_groups * EXPERTS_PER_GROUP

    nrow = s // CMP_STRIDE
    n_sel = s // SEL_LEN
    n_gate = n_heads * N_NSA_BRANCHES
    nsa_cols = nsa_w + 6 * kv_w + n_gate
    c_q = 2 * COL_BLOCK
    c_sc = c_q + nsa_cols
    c_mg = c_sc + 3 * COL_BLOCK
    gate_block = 2 * COL_BLOCK // d
    sc_block = (2 * COL_BLOCK + w_in.shape[2] - c_mg) // COL_BLOCK
    overlap_t = _overlap_t(n_sel, nrow)
    chunk = CMP_STRIDE * dh

    w_main, w_nsa = _pack_w_in(w_in, c_q, c_sc, c_mg)
    w_gate_bf, w_up_bf, w_down_bf = exp_w_gate.astype(BF16), exp_w_up.astype(BF16), exp_w_down.astype(BF16)
    y_moe = None
    for l in range(depth):
        proj3, qt5, kv_flat, kv_rows, v_t, gates_t, x = _inproj(
            x, None if y_moe is None else y_moe.reshape(b, s, d), mix_norm[l][None, :], w_main, w_nsa, l, kvh)

        y_rnn = _rnn_branch(
            proj3, rnn_conv_w[l], rnn_conv_b[l][None, :],
            _block_diag(rg_wa[l]).astype(BF16), rg_ba[l].reshape(1, rnn_w),
            _block_diag(rg_wx[l]).astype(BF16), rg_bx[l].reshape(1, rnn_w), rg_lambda[l][None, :])

        pe_flat = cmp_pe[l].reshape(2, 2, chunk)
        k_cmp, v_cmp, k_sel, k_win = _kvprep(
            kv_flat, kv_rows, pe_flat, cmp_w1[l].astype(BF16), cmp_b1[l][:, None, :],
            cmp_w2[l].astype(BF16), cmp_b2[l][:, None, :], k_norm[l])
        y_nsa = _nsa(qt5, k_cmp, v_cmp.transpose(0, 1, 3, 2), k_sel, k_win, v_t, gates_t,
                     q_norm[l][:, None], overlap_t)

        x = _merge(x, proj3, y_rnn, y_nsa, sc_conv_w[l], w_rnn_out[l].astype(BF16), w_nsa_out[l].astype(BF16),
                   w_sc_out[l].astype(BF16), w_out[l].astype(BF16), sc_block, gate_block)

        n_pad = ROUTER_ROWS - n_experts - n_groups
        w_router_t = jnp.concatenate(
            [router_expert_w[l].T, router_group_w[l].T, jnp.zeros((n_pad, d), F32)], axis=0).astype(BF16)
        b_router_t = jnp.concatenate([router_expert_b[l], router_group_b[l], jnp.zeros((n_pad,), F32)])[:, None]
        xn, slots, info_t = _router(x.reshape(n, d), ffn_norm[l][None, :], w_router_t, b_router_t, n_experts, n_groups)
        y_moe = _moe(xn, slots, info_t, w_gate_bf, w_up_bf, w_down_bf, l, n_groups,
                     x_res=x.reshape(n, d) if l == depth - 1 else None)
    return y_moe.reshape(b, s, d)
```

```python
import functools

import jax
import jax.numpy as jnp
from jax import lax
from jax.experimental import pallas as pl
from jax.experimental.pallas import tpu as pltpu

F32 = jnp.float32
BF16 = jnp.bfloat16

RMS_EPS = 1e-6
NEG_INF = -1e30
FORCE_SCORE = 1e6
RG_C = 8.0
HEAD_DIM = 64
NSA_GROUP = 4
N_NSA_BRANCHES = 3
CMP_LEN = 32
CMP_STRIDE = 16
SEL_LEN = 64
SEL_TOPK = 16
WINDOW = 512
Q_BLOCK = 512
EXPERTS_PER_GROUP = 8

LANES = 128
SUBLANES = 8
COL_BLOCK = 512
VMEM_LIMIT = 56 * 1024 * 1024
LOG2E = 1.4426950408889634


def _cparams(*sem):
    return pltpu.CompilerParams(dimension_semantics=sem, vmem_limit_bytes=VMEM_LIMIT)


def _rms(x, gain):
    return x * lax.rsqrt(jnp.mean(x * x, axis=-1, keepdims=True) + RMS_EPS) * gain


def _gelu_tanh(x):
    return 0.5 * x * (1.0 + jnp.tanh(0.7978845608028654 * (x + 0.044715 * x * x * x)))


def _dot(a, b):
    return jnp.dot(a, b, preferred_element_type=F32)


def _pack_kernel(w_ref, main_ref, nsa_ref, *, c_q, c_sc, c_mg):
    w = w_ref[...]
    n_mg = w.shape[1] - c_mg
    main_ref[:, 0:c_q] = w[:, 0:c_q].astype(BF16)
    main_ref[:, c_q:c_q + n_mg] = w[:, c_mg:].astype(BF16)
    main_ref[:, c_q + n_mg:] = w[:, c_sc:c_mg].astype(BF16)
    nsa_ref[:, 0:c_sc - c_q] = w[:, c_q:c_sc].astype(BF16)
    nsa_ref[:, c_sc - c_q:] = jnp.zeros((w.shape[0], nsa_ref.shape[1] - (c_sc - c_q)), BF16)


def _pack_w_in(w_in, c_q, c_sc, c_mg, tr=128):
    depth, d, win = w_in.shape
    w_main = win - (c_sc - c_q)
    w_nsa = -(-(c_sc - c_q) // LANES) * LANES
    return pl.pallas_call(
        functools.partial(_pack_kernel, c_q=c_q, c_sc=c_sc, c_mg=c_mg),
        grid=(depth, d // tr),
        in_specs=[pl.BlockSpec((None, tr, win), lambda l, i: (l, i, 0))],
        out_specs=[pl.BlockSpec((None, tr, w_main), lambda l, i: (l, i, 0)),
                   pl.BlockSpec((None, tr, w_nsa), lambda l, i: (l, i, 0))],
        out_shape=[jax.ShapeDtypeStruct((depth, d, w_main), BF16), jax.ShapeDtypeStruct((depth, d, w_nsa), BF16)],
        compiler_params=_cparams("parallel", "parallel"),
        name="pack_w_in",
    )(w_in)


def _inproj_kernel(*refs, has_y, kvh):
    if has_y:
        x_ref, y_ref, g_ref, w_ref, wn_ref, o_ref, qt_ref, kf_ref, kr_ref, vt_ref, gt_ref, xo_ref = refs[:-2]
        x = x_ref[...] + y_ref[...].astype(F32)
        xo_ref[...] = x
    else:
        x_ref, g_ref, w_ref, wn_ref, o_ref, qt_ref, kf_ref, kr_ref, vt_ref, gt_ref = refs[:-2]
        x = x_ref[...]
    xn = _rms(x, g_ref[...]).astype(BF16)
    for c in range(w_ref.shape[1] // COL_BLOCK):
        sl = slice(c * COL_BLOCK, (c + 1) * COL_BLOCK)
        o_ref[:, sl] = _dot(xn, w_ref[:, sl]).astype(BF16)

    tm = x.shape[0]
    dh = HEAD_DIM
    tq = Q_BLOCK
    nsa = _dot(xn, wn_ref[...])
    for pair in range(kvh * NSA_GROUP // 2):
        slab_t = nsa[:, pair * LANES:(pair + 1) * LANES].T
        for half in range(2):
            head = 2 * pair + half
            g, h = head // NSA_GROUP, head % NSA_GROUP
            for j in range(tm // tq):
                qt_ref[g, j, :, h * tq:(h + 1) * tq] = slab_t[half * dh:(half + 1) * dh,
                                                             j * tq:(j + 1) * tq].astype(BF16)
    kv0 = kvh * NSA_GROUP * dh
    ones_rows = jnp.where(lax.broadcasted_iota(jnp.int32, (2 * SUBLANES, tm), 0) == 0, 1.0, 0.0).astype(BF16)
    per_tile = LANES // dh
    for slab, (row_slot, col_slot) in enumerate(((None, None), (None, None), (0, None), (None, 0), (1, None), (None, 1))):
        v = nsa[:, kv0 + slab * kvh * dh:kv0 + (slab + 1) * kvh * dh]
        if row_slot is None and col_slot is None:
            stage = refs[-2 + slab]
            stage[...] = v
            for tile in range(CMP_STRIDE // per_tile):
                toks = [stage[pl.ds(tile * per_tile + k, tm // CMP_STRIDE, stride=CMP_STRIDE), :]
                        for k in range(per_tile)]
                for g in range(kvh):
                    kf_ref[slab, g, :, tile * LANES:(tile + 1) * LANES] = jnp.concatenate(
                        [t[:, g * dh:(g + 1) * dh] for t in toks], axis=1).astype(BF16)
        elif row_slot is not None:
            for g in range(kvh):
                kr_ref[row_slot, g] = v[:, g * dh:(g + 1) * dh].astype(BF16)
        else:
            v_t = v.T
            for g in range(kvh):
                vt_ref[col_slot, g, 0:dh, :] = v_t[g * dh:(g + 1) * dh].astype(BF16)
                vt_ref[col_slot, g, dh:dh + 2 * SUBLANES, :] = ones_rows
    g0 = kv0 + 6 * kvh * dh
    gates_t = nsa[:, g0:g0 + LANES].T
    per_group = NSA_GROUP * N_NSA_BRANCHES
    for g in range(kvh):
        gt_ref[g] = gates_t[g * per_group:(g + 1) * per_group].astype(BF16)


def _inproj(x3, y3, gain, w_main, w_nsa, layer, kvh, tm=512):
    b, s, d = x3.shape
    wp = w_main.shape[2]
    dh = HEAD_DIM
    row = pl.BlockSpec((None, tm, d), lambda bi, ti: (bi, ti, 0))
    has_y = y3 is not None
    lanes = NSA_GROUP * Q_BLOCK
    per_group = NSA_GROUP * N_NSA_BRANCHES
    out_specs = [
        pl.BlockSpec((None, tm, wp), lambda bi, ti: (bi, ti, 0)),
        pl.BlockSpec((None, kvh, tm // Q_BLOCK, dh, lanes), lambda bi, ti: (bi, 0, ti, 0, 0)),
        pl.BlockSpec((2, None, kvh, tm // CMP_STRIDE, CMP_STRIDE * dh), lambda bi, ti: (0, bi, 0, ti, 0)),
        pl.BlockSpec((2, None, kvh, tm, dh), lambda bi, ti: (0, bi, 0, ti, 0)),
        pl.BlockSpec((2, None, kvh, dh + 2 * SUBLANES, tm), lambda bi, ti: (0, bi, 0, 0, ti)),
        pl.BlockSpec((None, kvh, per_group, tm), lambda bi, ti: (bi, 0, 0, ti)),
    ] + ([row] if has_y else [])
    out_shape = [
        jax.ShapeDtypeStruct((b, s, wp), BF16),
        jax.ShapeDtypeStruct((b, kvh, s // Q_BLOCK, dh, lanes), BF16),
        jax.ShapeDtypeStruct((2, b, kvh, s // CMP_STRIDE, CMP_STRIDE * dh), BF16),
        jax.ShapeDtypeStruct((2, b, kvh, s, dh), BF16),
        jax.ShapeDtypeStruct((2, b, kvh, dh + 2 * SUBLANES, s), BF16),
        jax.ShapeDtypeStruct((b, kvh, per_group, s), BF16),
    ] + ([jax.ShapeDtypeStruct((b, s, d), F32)] if has_y else [])
    const = lambda a: pl.BlockSpec((None,) + a.shape[1:], lambda bi, ti: (layer, 0, 0), pipeline_mode=pl.Buffered(1))
    res = pl.pallas_call(
        functools.partial(_inproj_kernel, has_y=has_y, kvh=kvh),
        grid=(b, s // tm),
        in_specs=[row] + ([row] if has_y else []) + [pl.BlockSpec((1, d), lambda bi, ti: (0, 0)),
                                                    const(w_main), const(w_nsa)],
        out_specs=out_specs,
        out_shape=out_shape,
        scratch_shapes=[pltpu.VMEM((tm, kvh * dh), F32), pltpu.VMEM((tm, kvh * dh), F32)],
        compiler_params=_cparams("parallel", "parallel"),
        name="inproj",
    )(*([x3, y3] if has_y else [x3]), gain, w_main, w_nsa)
    return tuple(res[:6]) + ((res[6],) if has_y else (x3,))


def _rnn_kernel(x_ref, xp_ref, y_ref, cw_ref, cb_ref, wa_ref, ba_ref, wx_ref, bx_ref, lam_ref,
                o_ref, h_scr):
    ti = pl.program_id(1)
    ts = x_ref.shape[0]
    kw = cw_ref.shape[0]

    @pl.when(ti == 0)
    def _():
        h_scr[...] = jnp.zeros_like(h_scr)

    x = x_ref[...].astype(F32)
    halo = jnp.where(ti == 0, 0.0, xp_ref[...].astype(F32))
    groups_x = jnp.concatenate([halo, x], axis=0).reshape(ts // SUBLANES + 1, SUBLANES, x.shape[1])
    row_in = lax.broadcasted_iota(jnp.int32, (ts // SUBLANES, SUBLANES, x.shape[1]), 1)
    u = cb_ref[...] + cw_ref[kw - 1:kw, :] * groups_x[1:]
    for shift in range(1, kw):
        rot = pltpu.roll(groups_x, shift, axis=1)
        u = u + cw_ref[kw - 1 - shift:kw - shift, :] * jnp.where(row_in >= shift, rot[1:], rot[:-1])
    u = u.reshape(ts, x.shape[1])
    ub = u.astype(BF16)
    r = jax.nn.sigmoid(_dot(ub, wa_ref[...]) + ba_ref[...])
    gi = jax.nn.sigmoid(_dot(ub, wx_ref[...]) + bx_ref[...])
    lam = lam_ref[...]
    softplus_neg = jnp.maximum(-lam, 0.0) + jnp.log(1.0 + jnp.exp(-jnp.abs(lam)))
    log_a = -RG_C * r * softplus_neg
    a = jnp.exp(log_a)
    b = jnp.sqrt(1.0 - a * a) * (gi * u)

    a = a.reshape(ts // SUBLANES, SUBLANES, a.shape[1])
    b = b.reshape(a.shape)
    row_in_group = lax.broadcasted_iota(jnp.int32, a.shape, 1)
    d = 1
    while d < SUBLANES:
        keep = row_in_group >= d
        a_sh = pltpu.roll(a, d, axis=1)
        b_sh = pltpu.roll(b, d, axis=1)
        b = jnp.where(keep, a * b_sh + b, b)
        a = jnp.where(keep, a * a_sh, a)
        d *= 2
    h_prev = h_scr[...]
    groups = []
    for r in range(ts // SUBLANES):
        h_r = a[r] * h_prev + b[r]
        groups.append(h_r)
        h_prev = h_r[SUBLANES - 1:SUBLANES, :]
    h = jnp.concatenate(groups, axis=0)
    h_scr[...] = h_prev
    o_ref[...] = (_gelu_tanh(y_ref[...].astype(F32)) * h).astype(BF16)


def _rnn_branch(proj3, conv_w, conv_b, wa_bd, ba, wx_bd, bx, lam, ts=512):
    b, s, _ = proj3.shape
    w = conv_w.shape[1]
    nb = w // COL_BLOCK
    rows_per = ts // SUBLANES
    const = lambda shape: pl.BlockSpec(shape, lambda bi, ti: (0, 0))
    return pl.pallas_call(
        _rnn_kernel,
        grid=(b, s // ts),
        in_specs=[
            pl.BlockSpec((None, ts, w), lambda bi, ti: (bi, ti, 0)),
            pl.BlockSpec((None, SUBLANES, w), lambda bi, ti: (bi, jnp.maximum(ti * rows_per - 1, 0), 0)),
            pl.BlockSpec((None, ts, w), lambda bi, ti: (bi, ti, nb)),
            const(conv_w.shape), const((1, w)), const((w, w)), const((1, w)), const((w, w)), const((1, w)),
            const((1, w)),
        ],
        out_specs=pl.BlockSpec((None, ts, w), lambda bi, ti: (bi, ti, 0)),
        out_shape=jax.ShapeDtypeStruct((b, s, w), BF16),
        scratch_shapes=[pltpu.VMEM((1, w), F32)],
        compiler_params=_cparams("parallel", "arbitrary"),
        name="rnn_branch",
    )(proj3, proj3, proj3, conv_w, conv_b, wa_bd, ba, wx_bd, bx, lam)


def _kvprep_kernel(kcf_ref, vcf_ref, ks_ref, kw_ref, pe_ref, w1_ref, b1_ref, w2_ref, b2_ref, kn_ref,
                   kc_o, vc_o, ks_o, kw_o):
    half = kcf_ref.shape[1]
    nrow = kcf_ref.shape[0]

    def compress(flat_ref, i):
        xf = flat_ref[...].astype(F32)
        xa = (xf + pe_ref[i, 0:1, :]).astype(BF16)
        xb = (xf + pe_ref[i, 1:2, :]).astype(BF16)
        first = _dot(xa, w1_ref[i, 0:half, :])
        second = _dot(xb, w1_ref[i, half:2 * half, :])
        hid = first + pltpu.roll(second, nrow - 1, axis=0) + b1_ref[i]
        hid = _gelu_tanh(hid).astype(BF16)
        return _dot(hid, w2_ref[i]) + b2_ref[i]

    kc_o[...] = _rms(compress(kcf_ref, 0), kn_ref[0:1, :]).astype(BF16)
    vc_o[...] = compress(vcf_ref, 1).astype(BF16)
    ks = _rms(ks_ref[...].astype(F32), kn_ref[1:2, :])
    row = lax.broadcasted_iota(jnp.int32, ks.shape, 0)
    col = lax.broadcasted_iota(jnp.int32, ks.shape, 1)
    block_hot = jnp.where((row // SEL_LEN) % SUBLANES == col, 1.0, 0.0)
    ks_o[...] = jnp.concatenate([ks, block_hot], axis=1).astype(BF16)
    kw_o[...] = _rms(kw_ref[...].astype(F32), kn_ref[2:3, :]).astype(BF16)


def _kvprep(kv_flat, kv_rows, pe_flat, w1, b1, w2, b2, k_norm):
    _, b, g, nrow, half = kv_flat.shape
    flat = lambda k: pl.BlockSpec((None, None, None, nrow, half), lambda bi, gi: (k, bi, gi, 0, 0))
    s = kv_rows.shape[3]
    dh = kv_rows.shape[4]
    slot = lambda k: pl.BlockSpec((None, None, None, s, dh), lambda bi, gi: (k, bi, gi, 0, 0))
    bg = lambda shape: pl.BlockSpec((None, None) + shape, lambda bi, gi: (bi, gi, 0, 0))
    full = lambda a: pl.BlockSpec(a.shape, lambda bi, gi: (0,) * a.ndim)
    return pl.pallas_call(
        _kvprep_kernel,
        grid=(b, g),
        in_specs=[flat(0), flat(1), slot(0), slot(1),
                  full(pe_flat), full(w1), full(b1), full(w2), full(b2), full(k_norm)],
        out_specs=[bg((nrow, dh)), bg((nrow, dh)), bg((s, 2 * dh)), bg((s, dh))],
        out_shape=[jax.ShapeDtypeStruct((b, g, nrow, dh), BF16), jax.ShapeDtypeStruct((b, g, nrow, dh), BF16),
                   jax.ShapeDtypeStruct((b, g, s, 2 * dh), BF16), jax.ShapeDtypeStruct((b, g, s, dh), BF16)],
        compiler_params=_cparams("parallel", "parallel"),
        name="kvprep",
    )(kv_flat, kv_flat, kv_rows, kv_rows, pe_flat, w1, b1, w2, b2, k_norm)


def _nsa_kernel(qt_ref, kc_ref, vct_ref, ks_ref, vst_ref, kw_ref, vwt_ref, gt_ref, qn_ref, ov_ref, cb_ref, bb_ref,
                o_ref, sel_scr, *, key_tile):
    c = pl.program_id(2)
    tq = Q_BLOCK
    t0 = c * tq
    n_sel = ov_ref.shape[0]
    ncp = kc_ref.shape[0]
    top_k = min(SEL_TOPK, n_sel)
    lanes = NSA_GROUP * tq
    blocks_per_tile = key_tile // SEL_LEN

    qf = qt_ref[...].astype(F32)
    qn = qf * lax.rsqrt(jnp.mean(qf * qf, axis=0, keepdims=True) + RMS_EPS) * qn_ref[...]
    qs = (qn * (HEAD_DIM ** -0.5 * LOG2E)).astype(BF16)
    t_lane = t0 + (lax.broadcasted_iota(jnp.int32, (1, lanes), 1) & (tq - 1))

    n_tiles = (t0 + tq + key_tile - 1) // key_tile
    n_full = n_tiles - 1

    pad_rows = ks_ref.shape[1] - HEAD_DIM - 2 * SUBLANES
    assert blocks_per_tile == SUBLANES and pad_rows >= 0

    def scores(kt, bias_rows=None):
        k0 = pl.multiple_of(kt * key_tile, key_tile)
        if bias_rows is None:
            bias_rows = jnp.zeros((SUBLANES, lanes), F32)
        extra = jnp.concatenate([bias_rows, jnp.zeros((SUBLANES, lanes), F32)], axis=0).astype(BF16)
        q_aug = jnp.concatenate([qs, extra, jnp.zeros((pad_rows, lanes), BF16)], axis=0)
        return _dot(ks_ref[pl.ds(k0, key_tile), :], q_aug)

    w0 =pl.multiple_of(jnp.maximum(t0 - WINDOW, 0), tq)
    s_c = _dot(kc_ref[...], qs)
    hq = tq // 2
    wl_h = WINDOW + hq
    w_starts = (0, jnp.minimum(hq, t0 - w0))
    half = lambda x, k: jnp.concatenate(
        [x[:, h * tq + k * hq:h * tq + (k + 1) * hq] for h in range(NSA_GROUP)], axis=1)
    w_rows = [pl.ds(pl.multiple_of(w0 + w_starts[k], hq), wl_h) for k in range(2)]
    s_w = [_dot(kw_ref[w_rows[k], :], half(qs, k)) for k in range(2)]
    s_l = scores(n_full)

    off_c = pl.multiple_of(cb_ref.shape[0] - ncp - t0 // CMP_STRIDE, SUBLANES)
    s_c = s_c + cb_ref[pl.ds(off_c, ncp), :]
    e_c = jnp.exp2(s_c - jnp.max(s_c, axis=0, keepdims=True))
    inv_c = 1.0 / jnp.sum(e_c, axis=0, keepdims=True)
    p_c = e_c * jnp.where(t_lane >= CMP_LEN - 1, inv_c, 0.0)
    o_c = _dot(vct_ref[...], p_c.astype(BF16))

    j_i = lax.broadcasted_iota(jnp.int32, (n_sel, tq), 0)
    cur = (t0 + lax.broadcasted_iota(jnp.int32, (n_sel, tq), 1)) // SEL_LEN
    ps = p_c[:, 0:tq]
    for h in range(1, NSA_GROUP):
        ps = ps + p_c[:, h * tq:(h + 1) * tq]
    ps_hi = ps.astype(BF16)
    ps_lo = (ps - ps_hi.astype(F32)).astype(BF16)
    imp = _dot(ov_ref[...], ps_hi) + _dot(ov_ref[...], ps_lo)
    forced = (j_i == 0) | (j_i == cur) | (j_i == cur - 1)
    imp = jnp.where(forced, FORCE_SCORE, imp)
    imp = jnp.where(j_i <= cur, imp, -1.0)
    slabs = [imp[v * SUBLANES:(v + 1) * SUBLANES, :] for v in range(n_sel // SUBLANES)]
    j_slab = lax.broadcasted_iota(jnp.int32, (SUBLANES, tq), 0)
    cnts = [jnp.zeros((SUBLANES, tq), jnp.int32) for _ in slabs]
    for jp in range(n_sel):
        r = imp[jp:jp + 1, :]
        for v, slab in enumerate(slabs):
            lo = v * SUBLANES
            if lo > jp:
                before = r >= slab
            elif lo + SUBLANES - 1 <= jp:
                before = r > slab
            else:
                before = (r > slab) | ((r == slab) & (j_slab + lo > jp))
            cnts[v] = cnts[v] + before.astype(jnp.int32)
    cnt = jnp.concatenate(cnts, axis=0)
    sel_bias = jnp.where((cnt < top_k) & (imp >= 0.0), 0.0, NEG_INF)
    sel_scr[...] = jnp.concatenate([sel_bias] * NSA_GROUP, axis=1)

    off_w = pl.multiple_of(WINDOW - (t0 - w0), tq)
    o_w_half = []
    for k in range(2):
        b_rows = pl.ds(pl.multiple_of(off_w + w_starts[k], hq), wl_h)
        bias = jnp.concatenate([bb_ref[b_rows, h * tq + k * hq:h * tq + (k + 1) * hq] for h in range(NSA_GROUP)],
                               axis=1)
        s_k = s_w[k] + bias
        e_k = jnp.exp2(s_k - jnp.max(s_k, axis=0, keepdims=True))
        a_k = _dot(vwt_ref[:, w_rows[k]], e_k.astype(BF16))
        o_w_half.append(a_k[0:HEAD_DIM] * (1.0 / a_k[HEAD_DIM:HEAD_DIM + 1]))
    o_w = jnp.concatenate([o_w_half[k][:, h * hq:(h + 1) * hq] for h in range(NSA_GROUP) for k in range(2)], axis=1)

    def bias_rows(kt):
        return sel_scr[pl.ds(pl.multiple_of(kt * blocks_per_tile, SUBLANES), blocks_per_tile), :]

    def update(carry, kt, s):
        m_o, a_o = carry
        k0 = pl.multiple_of(kt * key_tile, key_tile)
        m_n = jnp.maximum(m_o, jnp.max(s, axis=0, keepdims=True))
        p = jnp.exp2(s - m_n).astype(BF16)
        a_n = jnp.exp2(m_o - m_n) * a_o + _dot(vst_ref[:, pl.ds(k0, key_tile)], p)
        return m_n, a_n

    def init():
        return jnp.full((1, lanes), NEG_INF, F32), jnp.zeros((vst_ref.shape[0], lanes), F32)

    off_l = pl.multiple_of(WINDOW - (t0 - n_full * key_tile), tq)
    last_rows = bias_rows(n_full)
    last_bias = jnp.concatenate(
        [jnp.broadcast_to(last_rows[jb:jb + 1, :], (SEL_LEN, lanes)) for jb in range(blocks_per_tile)], axis=0)
    c_first = update(init(), n_full, s_l + last_bias + bb_ref[pl.ds(off_l, key_tile), :])

    def pair_step(i, carry):
        c_a, c_b = carry
        s_a = scores(2 * i, bias_rows(2 * i))
        s_b = scores(2 * i + 1, bias_rows(2 * i + 1))
        return update(c_a, 2 * i, s_a), update(c_b, 2 * i + 1, s_b)

    c_a, c_b = lax.fori_loop(0, n_full // 2, pair_step, (c_first, init()))
    (m_a, a_a), (m_b, a_b) = c_a, lax.cond(
        n_full % 2 == 1,
        lambda c: update(c, n_full - 1, scores(n_full - 1, bias_rows(n_full - 1))),
        lambda c: c, c_b)
    m_ab = jnp.maximum(m_a, m_b)
    a_s = jnp.exp2(m_a - m_ab) * a_a + jnp.exp2(m_b - m_ab) * a_b
    o_s = a_s[0:HEAD_DIM] * (1.0 / a_s[HEAD_DIM:HEAD_DIM + 1])

    gates = jax.nn.sigmoid(gt_ref[...].astype(F32))
    for h in range(NSA_GROUP):
        hs = slice(h * tq, (h + 1) * tq)
        gb = h * N_NSA_BRANCHES
        o = (gates[gb:gb + 1, :] * o_c[:, hs] + gates[gb + 1:gb + 2, :] * o_s[:, hs]
             + gates[gb + 2:gb + 3, :] * o_w[:, hs])
        o_ref[:, h * HEAD_DIM:(h + 1) * HEAD_DIM] = o.T.astype(BF16)


def _band_bias(n_rows, lanes):
    i = jnp.arange(n_rows)[:, None]
    tl = (jnp.arange(lanes) % Q_BLOCK)[None, :]
    return jnp.where((tl < i) & (i <= tl + WINDOW), 0.0, NEG_INF).astype(F32)


def _cmp_bias(n_lead, ncp, lanes):
    i = jnp.arange(n_lead + ncp)[:, None]
    tl = (jnp.arange(lanes) % Q_BLOCK)[None, :]
    return jnp.where(CMP_STRIDE * (i - n_lead) + CMP_LEN - 1 <= tl, 0.0, NEG_INF).astype(F32)


def _nsa(qt5, k_cmp, v_cmp_t, k_sel, k_win, v_t, gates_t, q_norm_col, overlap_t, key_tile=512):
    b, g, nchunk, dh, lanes = qt5.shape
    s = k_sel.shape[2]
    ncp = k_cmp.shape[2]
    ng = gates_t.shape[2]
    n_sel = overlap_t.shape[0]
    assert key_tile <= WINDOW and key_tile % Q_BLOCK == 0 and s % key_tile == 0
    cmp_bias = _cmp_bias((s - Q_BLOCK) // CMP_STRIDE, ncp, lanes)
    band_bias = _band_bias(2 * WINDOW + Q_BLOCK, lanes)
    bg = lambda shape: pl.BlockSpec((None, None) + shape, lambda bi, gi, ci: (bi, gi, 0, 0))
    v_slot = lambda k: pl.BlockSpec((None, None, None) + v_t.shape[3:], lambda bi, gi, ci: (k, bi, gi, 0, 0))
    const = lambda a: pl.BlockSpec(a.shape, lambda bi, gi, ci: (0, 0), pipeline_mode=pl.Buffered(1))
    return pl.pallas_call(
        functools.partial(_nsa_kernel, key_tile=key_tile),
        grid=(b, g, nchunk),
        in_specs=[
            pl.BlockSpec((None, None, None, dh, lanes), lambda bi, gi, ci: (bi, gi, ci, 0, 0)),
            bg((ncp, dh)), bg((dh, ncp)), bg((s, k_sel.shape[3])), v_slot(0), bg((s, dh)), v_slot(1),
            pl.BlockSpec((None, None, ng, Q_BLOCK), lambda bi, gi, ci: (bi, gi, 0, ci)),
            pl.BlockSpec(q_norm_col.shape, lambda bi, gi, ci: (0, 0)),
            pl.BlockSpec(overlap_t.shape, lambda bi, gi, ci: (0, 0)),
            const(cmp_bias), const(band_bias),
        ],
        out_specs=pl.BlockSpec((None, Q_BLOCK, NSA_GROUP * dh), lambda bi, gi, ci: (bi, ci, gi)),
        out_shape=jax.ShapeDtypeStruct((b, s, g * NSA_GROUP * dh), BF16),
        scratch_shapes=[pltpu.VMEM((n_sel, lanes), F32)],
        compiler_params=_cparams("parallel", "parallel", "arbitrary"),
        name="nsa",
    )(qt5, k_cmp, v_cmp_t, k_sel, v_t, k_win, v_t, gates_t, q_norm_col, overlap_t, cmp_bias, band_bias)


def _merge_kernel(x_ref, yr_ref, yn_ref, sb_ref, sc_ref, sh_ref, scp_ref, shp_ref, g0_ref, g1_ref, g2_ref,
                  cw_ref, wr_ref, wn_ref, ws_ref, wo_ref, o_ref):
    ti = pl.program_id(1)
    tm = x_ref.shape[0]
    kw = cw_ref.shape[0]
    prod = sc_ref[...].astype(F32) * sh_ref[...].astype(F32)
    halo = jnp.where(ti == 0, 0.0, scp_ref[...].astype(F32) * shp_ref[...].astype(F32))
    groups_p = jnp.concatenate([halo, prod], axis=0).reshape(tm // SUBLANES + 1, SUBLANES, prod.shape[1])
    row_in = lax.broadcasted_iota(jnp.int32, (tm // SUBLANES, SUBLANES, prod.shape[1]), 1)
    conv = cw_ref[kw - 1:kw, :] * groups_p[1:]
    for shift in range(1, kw):
        rot = pltpu.roll(groups_p, shift, axis=1)
        conv = conv + cw_ref[kw - 1 - shift:kw - shift, :] * jnp.where(row_in >= shift, rot[1:], rot[:-1])
    y_sc = (sb_ref[...].astype(F32) * conv.reshape(tm, prod.shape[1])).astype(BF16)
    merged = (jax.nn.sigmoid(g0_ref[...].astype(F32)) * _dot(yr_ref[...], wr_ref[...])
              + jax.nn.sigmoid(g1_ref[...].astype(F32)) * _dot(yn_ref[...], wn_ref[...])
              + jax.nn.sigmoid(g2_ref[...].astype(F32)) * _dot(y_sc, ws_ref[...]))
    o_ref[...] = x_ref[...] + _dot(merged.astype(BF16), wo_ref[...])


def _merge(x3, proj3, y_rnn, y_nsa, sc_conv_w, w_rnn_out, w_nsa_out, w_sc_out, w_out, sc_block, gate_block, tm=512):
    b, s, d = x3.shape
    w = y_rnn.shape[2]
    rows_per = tm // SUBLANES
    tile = lambda width, col: pl.BlockSpec((None, tm, width), lambda bi, ti: (bi, ti, col))
    prev = lambda col: pl.BlockSpec((None, SUBLANES, w),
                                    lambda bi, ti: (bi, jnp.maximum(ti * rows_per - 1, 0), col))
    const = lambda a: pl.BlockSpec(a.shape, lambda bi, ti: (0, 0))
    return pl.pallas_call(
        _merge_kernel,
        grid=(b, s // tm),
        in_specs=[
            tile(d, 0), tile(w, 0), tile(w, 0),
            tile(w, sc_block), tile(w, sc_block + 1), tile(w, sc_block + 2),
            prev(sc_block + 1), prev(sc_block + 2),
            tile(d, gate_block), tile(d, gate_block + 1), tile(d, gate_block + 2),
            const(sc_conv_w), const(w_rnn_out), const(w_nsa_out), const(w_sc_out), const(w_out),
        ],
        out_specs=tile(d, 0),
        out_shape=jax.ShapeDtypeStruct((b, s, d), F32),
        compiler_params=_cparams("parallel", "parallel"),
        name="merge",
    )(x3, y_rnn, y_nsa, proj3, proj3, proj3, proj3, proj3, proj3, proj3, proj3,
      sc_conv_w, w_rnn_out, w_nsa_out, w_sc_out, w_out)


MOE_BLOCK = 1024
MOE_ROW_TILE = 304
SLOT_IND = EXPERTS_PER_GROUP
SLOT_POS = EXPERTS_PER_GROUP + 1


ROUTER_ROWS = 48


def _route_t(logits, n_experts, n_groups):
    row_i = lax.broadcasted_iota(jnp.int32, logits.shape, 0)
    row = row_i.astype(F32)
    big = float(logits.shape[0])
    is_g = (row_i >= n_experts) & (row_i < n_experts + n_groups)
    gl = jnp.where(is_g, logits, NEG_INF)
    gmax = jnp.max(gl, axis=0, keepdims=True)
    grp = jnp.min(jnp.where(is_g & (gl == gmax), row - n_experts, big), axis=0, keepdims=True)
    p_grp = 1.0 / jnp.sum(jnp.where(is_g, jnp.exp(gl - gmax), 0.0), axis=0, keepdims=True)
    in_g = (row_i < n_experts) & ((row_i // EXPERTS_PER_GROUP).astype(F32) == grp)
    el = jnp.where(in_g, logits, NEG_INF)
    ee = jnp.where(in_g, jnp.exp(el - jnp.max(el, axis=0, keepdims=True)), 0.0)
    ep = ee / jnp.sum(ee, axis=0, keepdims=True)
    top1 = jnp.max(jnp.where(in_g, ep, -1.0), axis=0, keepdims=True)
    i1 = jnp.min(jnp.where(in_g & (ep == top1), row, big), axis=0, keepdims=True)
    rest = jnp.where(in_g & (row != i1), ep, -1.0)
    top2 = jnp.max(rest, axis=0, keepdims=True)
    i2 = jnp.min(jnp.where(rest == top2, row, big), axis=0, keepdims=True)
    denom = top1 + top2
    cw = (jnp.where(row == i1, p_grp * top1 / denom, 0.0)
          + jnp.where(row == i2, p_grp * top2 / denom, 0.0))
    return cw, grp


def _router_kernel(x_ref, g_ref, wr_ref, br_ref, tri_ref, xn_ref, slot_ref, info_t_ref, *, n_experts, n_groups):
    xn = _rms(x_ref[...], g_ref[...]).astype(BF16)
    xn_ref[...] = xn
    tb = xn.shape[0]
    logits = lax.dot_general(wr_ref[...], xn, (((1,), (1,)), ((), ())), preferred_element_type=F32) + br_ref[...]
    cw, grp = _route_t(logits, n_experts, n_groups)
    g_row = lax.broadcasted_iota(jnp.int32, (SUBLANES, tb), 0)
    ind = jnp.where((g_row.astype(F32) == grp) & (g_row < n_groups), 1.0, 0.0)
    pos = _dot(ind.astype(BF16), tri_ref[...])
    info_t_ref[...] = jnp.where(g_row < n_groups, ind, pltpu.roll(pos, n_groups, axis=0))
    pad = jnp.zeros((LANES - 2 * SUBLANES, tb), F32)
    extra_row = lax.broadcasted_iota(jnp.int32, (SUBLANES, tb), 0)
    for g in range(n_groups):
        w_g = cw[g * EXPERTS_PER_GROUP:(g + 1) * EXPERTS_PER_GROUP]
        extra = jnp.where(extra_row == SLOT_IND - EXPERTS_PER_GROUP, ind[g:g + 1],
                          jnp.where(extra_row == SLOT_POS - EXPERTS_PER_GROUP, pos[g:g + 1], 0.0))
        slot_ref[:, g * LANES:(g + 1) * LANES] = jnp.concatenate([w_g, extra, pad], axis=0).T


def _router(x2, gain, w_router_t, b_router_t, n_experts, n_groups):
    n, d = x2.shape
    tb = MOE_BLOCK
    assert EXPERTS_PER_GROUP == SUBLANES and n_groups <= SUBLANES // 2
    tri = (jnp.arange(tb)[:, None] < jnp.arange(tb)[None, :]).astype(BF16)
    const = lambda a: pl.BlockSpec(a.shape, lambda i: (0, 0))
    return pl.pallas_call(
        functools.partial(_router_kernel, n_experts=n_experts, n_groups=n_groups),
        grid=(n // tb,),
        in_specs=[pl.BlockSpec((tb, d), lambda i: (i, 0)), const(gain), const(w_router_t), const(b_router_t),
                  pl.BlockSpec(tri.shape, lambda i: (0, 0), pipeline_mode=pl.Buffered(1))],
        out_specs=[pl.BlockSpec((tb, d), lambda i: (i, 0)),
                   pl.BlockSpec((tb, n_groups * LANES), lambda i: (i, 0)),
                   pl.BlockSpec((2 * n_groups, tb), lambda i: (0, i))],
        out_shape=[jax.ShapeDtypeStruct((n, d), BF16),
                   jax.ShapeDtypeStruct((n, n_groups * LANES), F32),
                   jax.ShapeDtypeStruct((2 * n_groups, n), F32)],
        compiler_params=_cparams("parallel"),
        name="router",
    )(x2, gain, w_router_t, b_router_t, tri)


def _moe_kernel(*refs, n_groups, has_x):
    if has_x:
        xn_ref, slot_ref, info_t_ref, wg_ref, wu_ref, wd_ref, x_ref, o_ref = refs
    else:
        xn_ref, slot_ref, info_t_ref, wg_ref, wu_ref, wd_ref, o_ref = refs
    g = pl.program_id(1)
    tb = xn_ref.shape[0]
    rt = MOE_ROW_TILE

    @pl.when(g == 0)
    def _():
        o_ref[...] = x_ref[...] if has_x else jnp.zeros_like(o_ref)

    slot = slot_ref[...]
    ind_col = slot[:, SLOT_IND:SLOT_IND + 1] > 0.5
    pos_col = slot[:, SLOT_POS:SLOT_POS + 1]
    ind_row = info_t_ref[pl.ds(g, 1), :]
    pos_row = info_t_ref[pl.ds(n_groups + g, 1), :]
    n_rows = jnp.sum(ind_row).astype(jnp.int32)
    n_tiles = (n_rows + rt - 1) // rt
    slot_hi = slot.astype(BF16)
    slot_lo = (slot - slot_hi.astype(F32)).astype(BF16)
    row_id = lax.broadcasted_iota(jnp.int32, (rt, tb), 0).astype(F32)
    col_id = lax.broadcasted_iota(jnp.int32, (tb, rt), 1).astype(F32)

    def tile(t, _):
        base = (t * rt).astype(F32)
        pick = jnp.where((pos_row == row_id + base) & (ind_row > 0.5), 1.0, 0.0).astype(BF16)
        xg = _dot(pick, xn_ref[...]).astype(BF16)
        wc = _dot(pick, slot_hi) + _dot(pick, slot_lo)
        y = None
        for e in range(EXPERTS_PER_GROUP):
            h = jax.nn.silu(_dot(xg, wg_ref[e])) * _dot(xg, wu_ref[e]) * wc[:, e:e + 1]
            y_e = _dot(h.astype(BF16), wd_ref[e])
            y = y_e if y is None else y + y_e
        put = jnp.where((pos_col == col_id + base) & ind_col, 1.0, 0.0).astype(BF16)
        o_ref[...] += _dot(put, y.astype(BF16)).astype(o_ref.dtype)
        return 0

    lax.fori_loop(0, n_tiles, tile, 0)


def _moe(xn, slots, info_t, w_gate, w_up, w_down, layer, n_groups, x_res=None):
    n, d = xn.shape
    f = w_down.shape[2]
    tb = MOE_BLOCK
    epg = EXPERTS_PER_GROUP
    assert MOE_ROW_TILE % (2 * SUBLANES) == 0
    has_x = x_res is not None
    row = pl.BlockSpec((tb, d), lambda i, g: (i, 0))
    return pl.pallas_call(
        functools.partial(_moe_kernel, n_groups=n_groups, has_x=has_x),
        grid=(n // tb, n_groups),
        in_specs=[
            pl.BlockSpec((tb, d), lambda i, g: (i, 0)),
            pl.BlockSpec((tb, LANES), lambda i, g: (i, g)),
            pl.BlockSpec((2 * n_groups, tb), lambda i, g: (0, i)),
            pl.BlockSpec((None, epg, d, f), lambda i, g: (layer, g, 0, 0)),
            pl.BlockSpec((None, epg, d, f), lambda i, g: (layer, g, 0, 0)),
            pl.BlockSpec((None, epg, f, d), lambda i, g: (layer, g, 0, 0)),
        ] + ([row] if has_x else []),
        out_specs=row,
        out_shape=jax.ShapeDtypeStruct((n, d), F32 if has_x else BF16),
        compiler_params=_cparams("parallel", "arbitrary"),
        name="moe",
    )(xn, slots, info_t, w_gate, w_up, w_down, *([x_res] if has_x else []))


def _block_diag(w):
    nb, bs, _ = w.shape
    eye = jnp.eye(nb, dtype=w.dtype)
    return (eye[:, None, :, None] * w[:, :, None, :]).reshape(nb * bs, nb * bs)


def _overlap_t(n_sel, ncp):
    n = jnp.arange(ncp)
    j = jnp.arange(n_sel)
    c0 = n * CMP_STRIDE
    s0 = j * SEL_LEN
    ov = jnp.clip(jnp.minimum(c0[None, :] + CMP_LEN, s0[:, None] + SEL_LEN)
                  - jnp.maximum(c0[None, :], s0[:, None]), 0).astype(F32) / CMP_LEN
    return ov.astype(BF16)


def kernel(x, mix_norm, w_in, rnn_conv_w, rnn_conv_b, rg_wa, rg_ba, rg_wx, rg_bx, rg_lambda, cmp_pe, cmp_w1, cmp_b1, cmp_w2, cmp_b2, q_norm, k_norm, sc_conv_w, w_rnn_out, w_nsa_out, w_sc_out, w_out, ffn_norm, router_group_w, router_group_b, router_expert_w, router_expert_b, exp_w_gate, exp_w_up, exp_w_down):
    b, s, d = x.shape
    depth = w_in.shape[0]
    rnn_w = rnn_conv_w.shape[2]
    sc_w = sc_conv_w.shape[2]
    nsa_w = w_nsa_out.shape[1]
    dh = q_norm.shape[1]
    n_heads = nsa_w // dh
    kvh = n_heads // NSA_GROUP
    kv_w = kvh * dh
    n_groups = router_group_w.shape[2]
    n_experts = router_expert_w.shape[2]
    n = b * s
    assert rnn_w == COL_BLOCK and sc_w == COL_BLOCK and nsa_w == COL_BLOCK and dh == HEAD_DIM
    assert 4 * kv_w == COL_BLOCK and d % COL_BLOCK == 0
    assert s % COL_BLOCK == 0 and s >= WINDOW + Q_BLOCK and s % SEL_LEN == 0
    assert n_experts + n_groups <= ROUTER_ROWS and n_experts == n_groups * EXPERTS_PER_GROUP

    nrow = s // CMP_STRIDE
    n_sel = s // SEL_LEN
    n_gate = n_heads * N_NSA_BRANCHES
    nsa_cols = nsa_w + 6 * kv_w + n_gate
    c_q = 2 * COL_BLOCK
    c_sc = c_q + nsa_cols
    c_mg = c_sc + 3 * COL_BLOCK
    gate_block = 2 * COL_BLOCK // d
    sc_block = (2 * COL_BLOCK + w_in.shape[2] - c_mg) // COL_BLOCK
    overlap_t = _overlap_t(n_sel, nrow)
    chunk = CMP_STRIDE * dh

    w_main, w_nsa = _pack_w_in(w_in, c_q, c_sc, c_mg)
    w_gate_bf, w_up_bf, w_down_bf = exp_w_gate.astype(BF16), exp_w_up.astype(BF16), exp_w_down.astype(BF16)
    y_moe = None
    for l in range(depth):
        proj3, qt5, kv_flat, kv_rows, v_t, gates_t, x = _inproj(
            x, None if y_moe is None else y_moe.reshape(b, s, d), mix_norm[l][None, :], w_main, w_nsa, l, kvh)

        y_rnn = _rnn_branch(
            proj3, rnn_conv_w[l], rnn_conv_b[l][None, :],
            _block_diag(rg_wa[l]).astype(BF16), rg_ba[l].reshape(1, rnn_w),
            _block_diag(rg_wx[l]).astype(BF16), rg_bx[l].reshape(1, rnn_w), rg_lambda[l][None, :])

        pe_flat = cmp_pe[l].reshape(2, 2, chunk)
        k_cmp, v_cmp, k_sel, k_win = _kvprep(
            kv_flat, kv_rows, pe_flat, cmp_w1[l].astype(BF16), cmp_b1[l][:, None, :],
            cmp_w2[l].astype(BF16), cmp_b2[l][:, None, :], k_norm[l])
        y_nsa = _nsa(qt5, k_cmp, v_cmp.transpose(0, 1, 3, 2), k_sel, k_win, v_t, gates_t,
                     q_norm[l][:, None], overlap_t)

        x = _merge(x, proj3, y_rnn, y_nsa, sc_conv_w[l], w_rnn_out[l].astype(BF16), w_nsa_out[l].astype(BF16),
                   w_sc_out[l].astype(BF16), w_out[l].astype(BF16), sc_block, gate_block)

        n_pad = ROUTER_ROWS - n_experts - n_groups
        w_router_t = jnp.concatenate(
            [router_expert_w[l].T, router_group_w[l].T, jnp.zeros((n_pad, d), F32)], axis=0).astype(BF16)
        b_router_t = jnp.concatenate([router_expert_b[l], router_group_b[l], jnp.zeros((n_pad,), F32)])[:, None]
        xn, slots, info_t = _router(x.reshape(n, d), ffn_norm[l][None, :], w_router_t, b_router_t, n_experts, n_groups)
        y_moe = _moe(xn, slots, info_t, w_gate_bf, w_up_bf, w_down_bf, l, n_groups,
                     x_res=x.reshape(n, d) if l == depth - 1 else None)
    return y_moe.reshape(b, s, d)
```
